```python
import math
import jax, jax.numpy as jnp
from jax import lax
import numpy as np

D_MODEL = 1024
BATCH = 8
SEQ = 8192
DEPTH = 1

CHUNK = 64
GMLP_BLOCK = 128
GMLP_WIDTH = D_MODEL
GMLP_GROUPS = 8
GMLP_GROUP_DIM = GMLP_WIDTH // GMLP_GROUPS
DIFF_HEAD_DIM = 64
DIFF_HEADS = D_MODEL // (2 * DIFF_HEAD_DIM)
DIFF_QK_WIDTH = DIFF_HEADS * 2 * DIFF_HEAD_DIM
DIFF_V_WIDTH = DIFF_HEADS * 2 * DIFF_HEAD_DIM
FF_WIDTH = 4 * D_MODEL
PLE_DIM = 256
ROPE_THETA = 10000.0
Q_BLOCK = 128
EPS = 1e-6

IN_WIDTHS = (GMLP_WIDTH, GMLP_WIDTH, DIFF_QK_WIDTH, DIFF_QK_WIDTH, DIFF_V_WIDTH, D_MODEL, D_MODEL)
IN_WIDTH = sum(IN_WIDTHS)
IN_SPLITS = tuple(int(v) for v in np.cumsum(IN_WIDTHS)[:-1])

kernel_name = "hybrid_gmlp_diffattn_block"


def rms_norm(x, gain):
    xf = x.astype(jnp.float32)
    y = xf * lax.rsqrt(jnp.mean(xf * xf, axis=-1, keepdims=True) + EPS)
    return (y * gain.astype(jnp.float32)).astype(x.dtype)


def layer_norm(x, gain, bias):
    xf = x.astype(jnp.float32)
    mu = jnp.mean(xf, axis=-1, keepdims=True)
    var = jnp.mean(jnp.square(xf - mu), axis=-1, keepdims=True)
    y = (xf - mu) * lax.rsqrt(var + EPS)
    return (y * gain.astype(jnp.float32) + bias.astype(jnp.float32)).astype(x.dtype)


def rope_tables(seq):
    pos = jnp.arange(seq, dtype=jnp.float32)
    inv = 1.0 / (ROPE_THETA ** (jnp.arange(0, DIFF_HEAD_DIM, 2, dtype=jnp.float32) / DIFF_HEAD_DIM))
    ang = pos[:, None] * inv[None, :]
    ang = jnp.concatenate([ang, ang], axis=-1)
    return jnp.cos(ang), jnp.sin(ang)


def apply_rope(t, cos, sin):
    tf = t.astype(jnp.float32)
    half = DIFF_HEAD_DIM // 2
    rot = jnp.concatenate([-tf[..., half:], tf[..., :half]], axis=-1)
    c = cos[None, :, None, None, :]
    s = sin[None, :, None, None, :]
    return (tf * c + rot * s).astype(t.dtype)


def gmlp_spatial_gate(u, v, ln_g, ln_b, w_s, b_s):
    B, S, _ = v.shape
    vn = layer_norm(v, ln_g, ln_b)
    vb = vn.reshape(B, S // GMLP_BLOCK, GMLP_BLOCK, GMLP_GROUPS, GMLP_GROUP_DIM)
    t_chunk = jnp.arange(GMLP_BLOCK) // CHUNK
    allowed = t_chunk[None, :] <= t_chunk[:, None]
    w = jnp.where(allowed[None], w_s, jnp.zeros_like(w_s))
    mixed = jnp.einsum('gts,bnsgc->bntgc', w, vb) + b_s.T[None, None, :, :, None]
    return u * mixed.reshape(B, S, GMLP_WIDTH)


def diff_attention(q, k, v, lam):
    B, S, H, _, Dh = q.shape
    nb = S // Q_BLOCK
    qs = (q * (Dh ** -0.5)).reshape(B, nb, Q_BLOCK, H, 2, Dh).transpose(1, 0, 2, 3, 4, 5)
    key_chunk = jnp.arange(S) // CHUNK

    def one_block(args):
        i, qb = args
        q_chunk = (i * Q_BLOCK + jnp.arange(Q_BLOCK)) // CHUNK
        allowed = key_chunk[None, :] <= q_chunk[:, None]
        s = jnp.einsum('bqhcd,bkhcd->bhcqk', qb, k).astype(jnp.float32)
        s = jnp.where(allowed, s, -jnp.inf)
        pr = jax.nn.softmax(s, axis=-1)
        a = pr[:, :, 0] - lam * pr[:, :, 1]
        return jnp.einsum('bhqk,bkhe->bqhe', a.astype(v.dtype), v)

    out = lax.map(one_block, (jnp.arange(nb), qs))
    return out.transpose(1, 0, 2, 3, 4).reshape(B, S, H, 2 * Dh)


def setup_inputs(seed: int = 0) -> dict:
    key = jax.random.key(seed)
    ks = jax.random.split(key, 26)
    f32 = jnp.float32

    def nrm(k, shape, scale):
        return jax.random.normal(k, shape, f32) * scale

    def gain(k, n):
        return 1.0 + 0.05 * jax.random.normal(k, (DEPTH, n), f32)

    return {
        "x": nrm(ks[0], (BATCH, SEQ, D_MODEL), 1.0),
        "p": nrm(ks[1], (DEPTH, BATCH, SEQ, PLE_DIM), 1.0),
        "norm_pre_mix": gain(ks[2], D_MODEL),
        "w_in": nrm(ks[3], (DEPTH, D_MODEL, IN_WIDTH), D_MODEL ** -0.5),
        "ln_v_gain": gain(ks[4], GMLP_WIDTH),
        "ln_v_bias": nrm(ks[5], (DEPTH, GMLP_WIDTH), 0.02),
        "w_spatial": nrm(ks[6], (DEPTH, GMLP_GROUPS, GMLP_BLOCK, GMLP_BLOCK), 0.5 * GMLP_BLOCK ** -0.5),
        "b_spatial": 1.0 + nrm(ks[7], (DEPTH, GMLP_GROUPS, GMLP_BLOCK), 0.1),
        "lambda_q1": nrm(ks[8], (DEPTH, DIFF_HEAD_DIM), 0.1),
        "lambda_k1": nrm(ks[9], (DEPTH, DIFF_HEAD_DIM), 0.1),
        "lambda_q2": nrm(ks[10], (DEPTH, DIFF_HEAD_DIM), 0.1),
        "lambda_k2": nrm(ks[11], (DEPTH, DIFF_HEAD_DIM), 0.1),
        "subln_gain": gain(ks[12], 2 * DIFF_HEAD_DIM),
        "w_branch_a": nrm(ks[13], (DEPTH, GMLP_WIDTH, D_MODEL), GMLP_WIDTH ** -0.5),
        "w_branch_b": nrm(ks[14], (DEPTH, DIFF_V_WIDTH, D_MODEL), DIFF_V_WIDTH ** -0.5),
        "w_out": nrm(ks[15], (DEPTH, D_MODEL, D_MODEL), D_MODEL ** -0.5),
        "norm_post_mix": gain(ks[16], D_MODEL),
        "norm_pre_ffn": gain(ks[17], D_MODEL),
        "w_ff1": nrm(ks[18], (DEPTH, D_MODEL, FF_WIDTH), D_MODEL ** -0.5),
        "w_ff2": nrm(ks[19], (DEPTH, FF_WIDTH, D_MODEL), FF_WIDTH ** -0.5),
        "norm_post_ffn": gain(ks[20], D_MODEL),
        "w_ple_proj": nrm(ks[21], (DEPTH, PLE_DIM, D_MODEL), PLE_DIM ** -0.5),
        "w_ple_gate": nrm(ks[22], (DEPTH, D_MODEL, D_MODEL), D_MODEL ** -0.5),
        "b_ple_gate": nrm(ks[23], (DEPTH, D_MODEL), 0.02),
        "norm_post_ple": gain(ks[24], D_MODEL),
    }


def reference(x, p, norm_pre_mix, w_in, ln_v_gain, ln_v_bias, w_spatial, b_spatial,
              lambda_q1, lambda_k1, lambda_q2, lambda_k2, subln_gain,
              w_branch_a, w_branch_b, w_out, norm_post_mix, norm_pre_ffn,
              w_ff1, w_ff2, norm_post_ffn, w_ple_proj, w_ple_gate, b_ple_gate,
              norm_post_ple):
    B, S, _ = x.shape
    cos, sin = rope_tables(S)
    h = x
    for i in range(DEPTH):
        n = rms_norm(h, norm_pre_mix[i])
        z = n @ w_in[i]
        u, v, q, k, va, ga, gb = jnp.split(z, IN_SPLITS, axis=-1)

        ya = gmlp_spatial_gate(jax.nn.gelu(u), jax.nn.gelu(v), ln_v_gain[i], ln_v_bias[i],
                               w_spatial[i], b_spatial[i])

        q = apply_rope(q.reshape(B, S, DIFF_HEADS, 2, DIFF_HEAD_DIM), cos, sin)
        k = apply_rope(k.reshape(B, S, DIFF_HEADS, 2, DIFF_HEAD_DIM), cos, sin)
        va = va.reshape(B, S, DIFF_HEADS, 2 * DIFF_HEAD_DIM)
        lambda_init = 0.8 - 0.6 * math.exp(-0.3 * i)
        lam = (jnp.exp(jnp.sum(lambda_q1[i].astype(jnp.float32) * lambda_k1[i].astype(jnp.float32)))
               - jnp.exp(jnp.sum(lambda_q2[i].astype(jnp.float32) * lambda_k2[i].astype(jnp.float32)))
               + lambda_init)
        o = diff_attention(q, k, va, lam)
        o = rms_norm(o, subln_gain[i]) * (1.0 - lambda_init)
        yb = o.reshape(B, S, DIFF_V_WIDTH)

        merged = jax.nn.sigmoid(ga) * (ya @ w_branch_a[i]) + jax.nn.sigmoid(gb) * (yb @ w_branch_b[i])
        h = h + rms_norm(merged @ w_out[i], norm_post_mix[i])

        f = rms_norm(h, norm_pre_ffn[i]) @ w_ff1[i]
        f = jnp.square(jax.nn.relu(f)) @ w_ff2[i]
        h = h + rms_norm(f, norm_post_ffn[i])

        e = (p[i] @ w_ple_proj[i]) * jax.nn.sigmoid(h @ w_ple_gate[i] + b_ple_gate[i])
        h = h + rms_norm(e, norm_post_ple[i])
    return h
```

```python
import math
from functools import partial

import jax
import jax.numpy as jnp
from jax import lax
from jax.experimental import pallas as pl
from jax.experimental.pallas import tpu as pltpu

D_MODEL = 1024
CHUNK = 64
GMLP_BLOCK = 128
GMLP_GROUPS = 8
HEAD_DIM = 64
HEADS = 8
HEAD_WIDTH = 2 * HEAD_DIM
FF_WIDTH = 4 * D_MODEL
PLE_DIM = 256
ROPE_THETA = 10000.0
EPS = 1e-6
LAMBDA_INIT = 0.8 - 0.6 * math.exp(-0.3 * 0)

MIX_WIDTH = 5 * D_MODEL
V7X_VMEM_LIMIT_BYTES = 56 * 1024 * 1024

MIX_TOKENS = 256
ATTN_Q = 512
ATTN_K = 512

F32 = jnp.float32
BF16 = jnp.bfloat16


def _rms(x, gain):
    return x * lax.rsqrt(jnp.mean(x * x, axis=-1, keepdims=True) + EPS) * gain


def _dot(a, b):
    return jnp.dot(a, b, preferred_element_type=F32)


def _resident(shape):
    zeros = (0,) * len(shape)
    return pl.BlockSpec(shape, lambda *_: zeros, pipeline_mode=pl.Buffered(1))


def _mix_in_kernel(x_ref, g_ref, w_ref, lng_ref, lnb_ref, wsp_ref, bsp_ref, cos_ref, sin_ref,
                   ya_ref, q_ref, k_ref, va_ref):
    tm = x_ref.shape[0]
    n = _rms(x_ref[...], g_ref[...]).astype(BF16)

    va_ref[...] = _dot(n, w_ref[:, 4 * D_MODEL:5 * D_MODEL]).astype(BF16)

    cos = cos_ref[...]
    sin = sin_ref[...]
    lane = lax.broadcasted_iota(jnp.int32, (tm, HEAD_WIDTH), 1)
    low_half = (lane % HEAD_DIM) < (HEAD_DIM // 2)
    for base, out_ref, scale in ((2 * D_MODEL, q_ref, HEAD_DIM ** -0.5), (3 * D_MODEL, k_ref, None)):
        t = _dot(n, w_ref[:, base:base + D_MODEL])
        for h in range(HEADS):
            th = t[:, h * HEAD_WIDTH:(h + 1) * HEAD_WIDTH]
            rot = jnp.where(low_half,
                            pltpu.roll(th, HEAD_WIDTH - HEAD_DIM // 2, 1),
                            pltpu.roll(th, HEAD_DIM // 2, 1))
            r = th * cos + rot * sin
            if scale is not None:
                r = r * scale
            out_ref[:, h * HEAD_WIDTH:(h + 1) * HEAD_WIDTH] = r.astype(BF16)

    gu = jax.nn.gelu(_dot(n, w_ref[:, 0:D_MODEL]))
    gv = jax.nn.gelu(_dot(n, w_ref[:, D_MODEL:2 * D_MODEL]))
    mu = jnp.mean(gv, axis=-1, keepdims=True)
    cen = gv - mu
    var = jnp.mean(cen * cen, axis=-1, keepdims=True)
    vn = (cen * lax.rsqrt(var + EPS) * lng_ref[...] + lnb_ref[...]).astype(BF16)

    row = lax.broadcasted_iota(jnp.int32, (GMLP_BLOCK, GMLP_BLOCK), 0)
    col = lax.broadcasted_iota(jnp.int32, (GMLP_BLOCK, GMLP_BLOCK), 1)
    allowed = (col // CHUNK) <= (row // CHUNK)
    gw = D_MODEL // GMLP_GROUPS
    for g in range(GMLP_GROUPS):
        w_g = jnp.where(allowed, wsp_ref[g], 0.0).astype(BF16)
        cs = slice(g * gw, (g + 1) * gw)
        for blk in range(tm // GMLP_BLOCK):
            rs = slice(blk * GMLP_BLOCK, (blk + 1) * GMLP_BLOCK)
            mixed = _dot(w_g, vn[rs, cs]) + bsp_ref[:, cs]
            ya_ref[rs, cs] = (gu[rs, cs] * mixed).astype(BF16)


def _mix_in(x2, g_pre, w_mix, ln_g, ln_b, w_sp, b_sp_full, cos, sin_signed, seq):
    tokens = x2.shape[0]
    tm = MIX_TOKENS
    tiles_per_seq = seq // tm
    tok_spec = pl.BlockSpec((tm, D_MODEL), lambda t: (t, 0))
    pos_spec = pl.BlockSpec((tm, HEAD_WIDTH), lambda t: (t % tiles_per_seq, 0))
    out = jax.ShapeDtypeStruct((tokens, D_MODEL), BF16)
    return pl.pallas_call(
        _mix_in_kernel,
        grid=(tokens // tm,),
        in_specs=[
            tok_spec,
            _resident((1, D_MODEL)),
            _resident((D_MODEL, MIX_WIDTH)),
            _resident((1, D_MODEL)),
            _resident((1, D_MODEL)),
            _resident((GMLP_GROUPS, GMLP_BLOCK, GMLP_BLOCK)),
            _resident((GMLP_BLOCK, D_MODEL)),
            pos_spec,
            pos_spec,
        ],
        out_specs=[tok_spec, tok_spec, tok_spec, tok_spec],
        out_shape=[out, out, out, out],
        compiler_params=pltpu.CompilerParams(
            dimension_semantics=("arbitrary",), vmem_limit_bytes=V7X_VMEM_LIMIT_BYTES),
        name="mix_in",
    )(x2, g_pre, w_mix, ln_g, ln_b, w_sp, b_sp_full, cos, sin_signed)


def _diff_attn_kernel(q_ref, k_ref, v_ref, lam_ref, g_ref, o_ref):
    tq = q_ref.shape[0]
    tk = ATTN_K
    i = pl.program_id(2)

    q = q_ref[...]
    lane = lax.broadcasted_iota(jnp.int32, (tq, HEAD_WIDTH), 1)
    zero = jnp.zeros_like(q)
    qs = (jnp.where(lane < HEAD_DIM, q, zero), jnp.where(lane >= HEAD_DIM, q, zero))

    def scores(c, kb):
        return lax.dot_general(qs[c], kb, (((1,), (1,)), ((), ())), preferred_element_type=F32)

    start = pl.multiple_of(i * tq, tq)
    kb = k_ref[pl.ds(start, tk), :]
    vb = v_ref[pl.ds(start, tk), :]
    row = lax.broadcasted_iota(jnp.int32, (tq, tk), 0)
    col = lax.broadcasted_iota(jnp.int32, (tq, tk), 1)
    allowed = (col // CHUNK) <= (row // CHUNK)
    carry = []
    for c in range(2):
        s = jnp.where(allowed, scores(c, kb), -jnp.inf)
        m = jnp.max(s, axis=-1, keepdims=True)
        p = jnp.exp(s - m)
        carry += [m, jnp.sum(p, axis=-1, keepdims=True), _dot(p.astype(BF16), vb)]

    def kv_step(j, carry):
        kstart = pl.multiple_of(j * tk, tk)
        kb = k_ref[pl.ds(kstart, tk), :]
        vb = v_ref[pl.ds(kstart, tk), :]
        new = []
        for c in range(2):
            m, l, acc = carry[3 * c:3 * c + 3]
            s = scores(c, kb)
            m_new = jnp.maximum(m, jnp.max(s, axis=-1, keepdims=True))
            alpha = jnp.exp(m - m_new)
            p = jnp.exp(s - m_new)
            new += [m_new,
                    alpha * l + jnp.sum(p, axis=-1, keepdims=True),
                    alpha * acc + _dot(p.astype(BF16), vb)]
        return tuple(new)

    _, l1, a1, _, l2, a2 = lax.fori_loop(0, i, kv_step, tuple(carry))

    lam_p = lam_ref[...]
    lam = (jnp.exp(jnp.sum(lam_p[0:1] * lam_p[1:2], axis=-1, keepdims=True))
           - jnp.exp(jnp.sum(lam_p[2:3] * lam_p[3:4], axis=-1, keepdims=True))
           + LAMBDA_INIT)
    o = a1 / l1 - lam * (a2 / l2)
    o_ref[...] = (_rms(o, g_ref[...]) * (1.0 - LAMBDA_INIT)).astype(BF16)


def _diff_attn(q, k, v, lam_params, subln_gain):
    batch, seq, _ = q.shape
    tq = ATTN_Q
    q_spec = pl.BlockSpec((None, tq, HEAD_WIDTH), lambda b, h, i: (b, i, h))
    kv_spec = pl.BlockSpec((None, seq, HEAD_WIDTH), lambda b, h, i: (b, 0, h))
    return pl.pallas_call(
        _diff_attn_kernel,
        grid=(batch, HEADS, seq // tq),
        in_specs=[
            q_spec, kv_spec, kv_spec,
            pl.BlockSpec((4, HEAD_DIM), lambda b, h, i: (0, 0)),
            pl.BlockSpec((1, HEAD_WIDTH), lambda b, h, i: (0, 0)),
        ],
        out_specs=q_spec,
        out_shape=jax.ShapeDtypeStruct((batch, seq, D_MODEL), BF16),
        compiler_params=pltpu.CompilerParams(
            dimension_semantics=("arbitrary", "arbitrary", "arbitrary"),
            vmem_limit_bytes=V7X_VMEM_LIMIT_BYTES),
        name="diff_attn",
    )(q, k, v, lam_params, subln_gain)


def _merge_ffn_kernel(x_ref, ya_ref, yb_ref, p_ref,
                      g_pre_ref, w_gate_ref, w_a_ref, w_b_ref, w_out_ref, g_post_mix_ref,
                      g_pre_ffn_ref, w_ff1_ref, w_ff2_ref, g_post_ffn_ref,
                      w_ple_proj_ref, w_ple_gate_ref, b_ple_gate_ref, g_post_ple_ref,
                      o_ref):
    x = x_ref[...]
    n = _rms(x, g_pre_ref[...]).astype(BF16)
    gates = jax.nn.sigmoid(_dot(n, w_gate_ref[...]))
    merged = (gates[:, :D_MODEL] * _dot(ya_ref[...], w_a_ref[...])
              + gates[:, D_MODEL:] * _dot(yb_ref[...], w_b_ref[...]))
    h = x + _rms(_dot(merged.astype(BF16), w_out_ref[...]), g_post_mix_ref[...])

    f = _dot(_rms(h, g_pre_ffn_ref[...]).astype(BF16), w_ff1_ref[...])
    f = jnp.square(jnp.maximum(f, 0.0))
    h = h + _rms(_dot(f.astype(BF16), w_ff2_ref[...]), g_post_ffn_ref[...])

    e = (_dot(p_ref[...].astype(BF16), w_ple_proj_ref[...])
         * jax.nn.sigmoid(_dot(h.astype(BF16), w_ple_gate_ref[...]) + b_ple_gate_ref[...]))
    o_ref[...] = h + _rms(e, g_post_ple_ref[...])


def _merge_ffn(x2, ya, yb, p2, params):
    tokens = x2.shape[0]
    tm = MIX_TOKENS
    tok_spec = pl.BlockSpec((tm, D_MODEL), lambda t: (t, 0))
    return pl.pallas_call(
        _merge_ffn_kernel,
        grid=(tokens // tm,),
        in_specs=[tok_spec, tok_spec, tok_spec, pl.BlockSpec((tm, PLE_DIM), lambda t: (t, 0))]
                 + [_resident(a.shape) for a in params],
        out_specs=tok_spec,
        out_shape=jax.ShapeDtypeStruct((tokens, D_MODEL), F32),
        compiler_params=pltpu.CompilerParams(
            dimension_semantics=("arbitrary",), vmem_limit_bytes=V7X_VMEM_LIMIT_BYTES),
        name="merge_ffn",
    )(x2, ya, yb, p2, *params)


def _rope_tables(seq):
    pos = jnp.arange(seq, dtype=F32)
    inv = 1.0 / (ROPE_THETA ** (jnp.arange(0, HEAD_DIM, 2, dtype=F32) / HEAD_DIM))
    ang = pos[:, None] * inv[None, :]
    cos = jnp.cos(ang)
    sin = jnp.sin(ang)
    cos_full = jnp.concatenate([cos, cos, cos, cos], axis=-1)
    sin_signed = jnp.concatenate([-sin, sin, -sin, sin], axis=-1)
    return cos_full, sin_signed


def kernel(x, p, norm_pre_mix, w_in, ln_v_gain, ln_v_bias, w_spatial, b_spatial, lambda_q1, lambda_k1, lambda_q2, lambda_k2, subln_gain, w_branch_a, w_branch_b, w_out, norm_post_mix, norm_pre_ffn, w_ff1, w_ff2, norm_post_ffn, w_ple_proj, w_ple_gate, b_ple_gate, norm_post_ple):
    batch, seq, d_model = x.shape
    depth = w_in.shape[0]
    assert d_model == D_MODEL and depth == 1
    assert seq % ATTN_Q == 0 and ATTN_Q == ATTN_K and seq % MIX_TOKENS == 0
    tokens = batch * seq
    x2 = x.reshape(tokens, D_MODEL)
    cos, sin_signed = _rope_tables(seq)

    def row(a):
        return a[0].reshape(1, -1).astype(F32)

    w_in_bf = w_in[0].astype(BF16)
    b_sp_full = jnp.repeat(b_spatial[0].T.astype(F32), D_MODEL // GMLP_GROUPS, axis=1)

    ya, q, k, va = _mix_in(
        x2, row(norm_pre_mix), w_in_bf[:, :MIX_WIDTH], row(ln_v_gain), row(ln_v_bias),
        w_spatial[0].astype(F32), b_sp_full, cos, sin_signed, seq)

    lam_params = jnp.concatenate([lambda_q1, lambda_k1, lambda_q2, lambda_k2], axis=0).astype(F32)
    yb = _diff_attn(q.reshape(batch, seq, D_MODEL), k.reshape(batch, seq, D_MODEL),
                    va.reshape(batch, seq, D_MODEL), lam_params, row(subln_gain))

    params = (
        row(norm_pre_mix), w_in_bf[:, MIX_WIDTH:], w_branch_a[0].astype(BF16),
        w_branch_b[0].astype(BF16), w_out[0].astype(BF16), row(norm_post_mix),
        row(norm_pre_ffn), w_ff1[0].astype(BF16), w_ff2[0].astype(BF16), row(norm_post_ffn),
        w_ple_proj[0].astype(BF16), w_ple_gate[0].astype(BF16), row(b_ple_gate), row(norm_post_ple),
    )
    out = _merge_ffn(x2, ya, yb.reshape(tokens, D_MODEL), p[0].reshape(tokens, PLE_DIM), params)
    return out.reshape(batch, seq, D_MODEL)
```

```python
import math

import jax
import jax.numpy as jnp
from jax import lax
from jax.experimental import pallas as pl
from jax.experimental.pallas import tpu as pltpu

D_MODEL = 1024
CHUNK = 64
GMLP_BLOCK = 128
GMLP_GROUPS = 8
HEAD_DIM = 64
HEADS = 8
HEAD_WIDTH = 2 * HEAD_DIM
FF_WIDTH = 4 * D_MODEL
PLE_DIM = 256
ROPE_THETA = 10000.0
EPS = 1e-6
LAMBDA_INIT = 0.8 - 0.6 * math.exp(-0.3 * 0)

MIX_WIDTH = 5 * D_MODEL
V7X_VMEM_LIMIT_BYTES = 56 * 1024 * 1024
BF16_SUBLANES = 16

TOKEN_TILE = 256

F32 = jnp.float32
BF16 = jnp.bfloat16


def _rms(x, gain):
    return x * lax.rsqrt(jnp.mean(x * x, axis=-1, keepdims=True) + EPS) * gain


def _dot(a, b):
    return jnp.dot(a, b, preferred_element_type=F32)


def _resident(shape):
    zeros = (0,) * len(shape)
    return pl.BlockSpec(shape, lambda *_: zeros, pipeline_mode=pl.Buffered(1))


def _mix_in_kernel(x_ref, g_ref, w_ref, lng_ref, lnb_ref, wsp_ref, bsp_ref, cos_ref, sin_ref,
                   ya_ref, qt_ref, k_ref, vt_ref):
    tm = x_ref.shape[0]
    n = _rms(x_ref[...], g_ref[...]).astype(BF16)

    va = _dot(n, w_ref[:, 4 * D_MODEL:5 * D_MODEL])
    for h in range(HEADS):
        hs = slice(h * HEAD_WIDTH, (h + 1) * HEAD_WIDTH)
        vt_ref[hs, :] = va[:, hs].T.astype(BF16)

    cos = cos_ref[...]
    sin = sin_ref[...]
    lane = lax.broadcasted_iota(jnp.int32, (tm, HEAD_WIDTH), 1)
    low_half = (lane % HEAD_DIM) < (HEAD_DIM // 2)

    def rope(t, h):
        th = t[:, h * HEAD_WIDTH:(h + 1) * HEAD_WIDTH]
        rot = jnp.where(low_half,
                        pltpu.roll(th, HEAD_WIDTH - HEAD_DIM // 2, 1),
                        pltpu.roll(th, HEAD_DIM // 2, 1))
        return th * cos + rot * sin

    t = _dot(n, w_ref[:, 2 * D_MODEL:3 * D_MODEL])
    for h in range(HEADS):
        hs = slice(h * HEAD_WIDTH, (h + 1) * HEAD_WIDTH)
        qt_ref[hs, :] = (rope(t, h) * HEAD_DIM ** -0.5).T.astype(BF16)
    t = _dot(n, w_ref[:, 3 * D_MODEL:4 * D_MODEL])
    for h in range(HEADS):
        hs = slice(h * HEAD_WIDTH, (h + 1) * HEAD_WIDTH)
        k_ref[:, hs] = rope(t, h).astype(BF16)

    gu = jax.nn.gelu(_dot(n, w_ref[:, 0:D_MODEL]))
    gv = jax.nn.gelu(_dot(n, w_ref[:, D_MODEL:2 * D_MODEL]))
    mu = jnp.mean(gv, axis=-1, keepdims=True)
    cen = gv - mu
    var = jnp.mean(cen * cen, axis=-1, keepdims=True)
    vn = (cen * lax.rsqrt(var + EPS) * lng_ref[...] + lnb_ref[...]).astype(BF16)

    row = lax.broadcasted_iota(jnp.int32, (GMLP_BLOCK, GMLP_BLOCK), 0)
    col = lax.broadcasted_iota(jnp.int32, (GMLP_BLOCK, GMLP_BLOCK), 1)
    allowed = (col // CHUNK) <= (row // CHUNK)
    gw = D_MODEL // GMLP_GROUPS
    for g in range(GMLP_GROUPS):
        w_g = jnp.where(allowed, wsp_ref[g], 0.0).astype(BF16)
        cs = slice(g * gw, (g + 1) * gw)
        for blk in range(tm // GMLP_BLOCK):
            rs = slice(blk * GMLP_BLOCK, (blk + 1) * GMLP_BLOCK)
            mixed = _dot(w_g, vn[rs, cs]) + bsp_ref[:, cs]
            ya_ref[rs, cs] = (gu[rs, cs] * mixed).astype(BF16)


def _mix_in(x2, g_pre, w_mix, ln_g, ln_b, w_sp, b_sp_full, cos, sin_signed, seq):
    tokens = x2.shape[0]
    tm = TOKEN_TILE
    tiles = tokens // tm
    tiles_per_seq = seq // tm
    tok_spec = pl.BlockSpec((tm, D_MODEL), lambda t: (t, 0))
    pos_spec = pl.BlockSpec((tm, HEAD_WIDTH), lambda t: (t % tiles_per_seq, 0))
    feat_spec = pl.BlockSpec((None, D_MODEL, tm), lambda t: (t, 0, 0))
    tok_out = jax.ShapeDtypeStruct((tokens, D_MODEL), BF16)
    feat_out = jax.ShapeDtypeStruct((tiles, D_MODEL, tm), BF16)
    return pl.pallas_call(
        _mix_in_kernel,
        grid=(tiles,),
        in_specs=[
            tok_spec,
            _resident((1, D_MODEL)),
            _resident((D_MODEL, MIX_WIDTH)),
            _resident((1, D_MODEL)),
            _resident((1, D_MODEL)),
            _resident((GMLP_GROUPS, GMLP_BLOCK, GMLP_BLOCK)),
            _resident((GMLP_BLOCK, D_MODEL)),
            pos_spec,
            pos_spec,
        ],
        out_specs=[tok_spec, feat_spec, tok_spec, feat_spec],
        out_shape=[tok_out, feat_out, tok_out, feat_out],
        compiler_params=pltpu.CompilerParams(
            dimension_semantics=("arbitrary",), vmem_limit_bytes=V7X_VMEM_LIMIT_BYTES),
        name="mix_in",
    )(x2, g_pre, w_mix, ln_g, ln_b, w_sp, b_sp_full, cos, sin_signed)


def _diff_attn_kernel(qt_ref, k_ref, vt_ref, lam_ref, g_ref, o_ref):
    t = qt_ref.shape[1]
    i = pl.program_id(2)

    qt = qt_ref[...]
    feat = lax.broadcasted_iota(jnp.int32, (HEAD_WIDTH, t), 0)
    zero = jnp.zeros_like(qt)
    qt2 = jnp.concatenate([jnp.where(feat < HEAD_DIM, qt, zero),
                           jnp.where(feat >= HEAD_DIM, qt, zero)], axis=1)
    ones = jnp.ones((BF16_SUBLANES, t), BF16)

    def qk(j):
        return _dot(k_ref[pl.ds(pl.multiple_of(j * t, t), t), :], qt2)

    def softmax_block(s, m):
        m_new = jnp.maximum(m, jnp.max(s, axis=0, keepdims=True))
        return m_new, jnp.exp(m - m_new), jnp.exp(s - m_new).astype(BF16)

    def pv(j, p, alpha, acc):
        vb = jnp.concatenate([vt_ref[j], ones], axis=0)
        return alpha * acc + _dot(vb, p)

    key = lax.broadcasted_iota(jnp.int32, (t, 2 * t), 0)
    qry = lax.broadcasted_iota(jnp.int32, (t, 2 * t), 1)
    allowed = (key // CHUNK) <= ((qry % t) // CHUNK)
    s_diag = jnp.where(allowed, qk(i), -jnp.inf)
    m = jnp.max(s_diag, axis=0, keepdims=True)
    p = jnp.exp(s_diag - m).astype(BF16)
    alpha = jnp.ones((1, 2 * t), F32)
    acc = jnp.zeros((HEAD_WIDTH + BF16_SUBLANES, 2 * t), F32)

    def kv_step(j, carry):
        s, p, alpha, m, acc, jp = carry
        acc = pv(jp, p, alpha, acc)
        m, alpha, p = softmax_block(s, m)
        s = qk(jnp.minimum(j + 1, i - 1))
        return s, p, alpha, m, acc, j

    _, p, alpha, _, acc, jp = lax.fori_loop(0, i, kv_step, (qk(0), p, alpha, m, acc, i))
    acc = pv(jp, p, alpha, acc)
    acc1 = acc[:, :t]
    acc2 = acc[:, t:]

    lam_p = lam_ref[...]
    lam = (jnp.exp(jnp.sum(lam_p[0:1] * lam_p[1:2], axis=-1, keepdims=True))
           - jnp.exp(jnp.sum(lam_p[2:3] * lam_p[3:4], axis=-1, keepdims=True))
           + LAMBDA_INIT)
    o = (acc1[:HEAD_WIDTH] / acc1[HEAD_WIDTH:HEAD_WIDTH + 1]
         - lam * (acc2[:HEAD_WIDTH] / acc2[HEAD_WIDTH:HEAD_WIDTH + 1]))
    y = o * lax.rsqrt(jnp.mean(o * o, axis=0, keepdims=True) + EPS)
    o_ref[...] = (y.T * g_ref[...] * (1.0 - LAMBDA_INIT)).astype(BF16)


def _diff_attn(qt, k, vt, lam_params, subln_gain):
    batch, seq, _ = k.shape
    t = TOKEN_TILE
    nt = seq // t
    return pl.pallas_call(
        _diff_attn_kernel,
        grid=(batch, HEADS, nt),
        in_specs=[
            pl.BlockSpec((None, None, HEAD_WIDTH, t), lambda b, h, i: (b, i, h, 0)),
            pl.BlockSpec((None, seq, HEAD_WIDTH), lambda b, h, i: (b, 0, h)),
            pl.BlockSpec((None, nt, HEAD_WIDTH, t), lambda b, h, i: (b, 0, h, 0)),
            pl.BlockSpec((4, HEAD_DIM), lambda b, h, i: (0, 0)),
            pl.BlockSpec((1, HEAD_WIDTH), lambda b, h, i: (0, 0)),
        ],
        out_specs=pl.BlockSpec((None, t, HEAD_WIDTH), lambda b, h, i: (b, i, h)),
        out_shape=jax.ShapeDtypeStruct((batch, seq, D_MODEL), BF16),
        compiler_params=pltpu.CompilerParams(
            dimension_semantics=("arbitrary", "arbitrary", "arbitrary"),
            vmem_limit_bytes=V7X_VMEM_LIMIT_BYTES),
        name="diff_attn",
    )(qt, k, vt, lam_params, subln_gain)


def _merge_ffn_kernel(x_ref, ya_ref, yb_ref, p_ref,
                      g_pre_ref, w_gate_ref, w_a_ref, w_b_ref, w_out_ref, g_post_mix_ref,
                      g_pre_ffn_ref, w_ff1_ref, w_ff2_ref, g_post_ffn_ref,
                      w_ple_proj_ref, w_ple_gate_ref, b_ple_gate_ref, g_post_ple_ref,
                      o_ref):
    x = x_ref[...]
    n = _rms(x, g_pre_ref[...]).astype(BF16)
    gates = jax.nn.sigmoid(_dot(n, w_gate_ref[...]))
    merged = (gates[:, :D_MODEL] * _dot(ya_ref[...], w_a_ref[...])
              + gates[:, D_MODEL:] * _dot(yb_ref[...], w_b_ref[...]))
    h = x + _rms(_dot(merged.astype(BF16), w_out_ref[...]), g_post_mix_ref[...])

    f = _dot(_rms(h, g_pre_ffn_ref[...]).astype(BF16), w_ff1_ref[...])
    f = jnp.square(jnp.maximum(f, 0.0))
    h = h + _rms(_dot(f.astype(BF16), w_ff2_ref[...]), g_post_ffn_ref[...])

    e = (_dot(p_ref[...].astype(BF16), w_ple_proj_ref[...])
         * jax.nn.sigmoid(_dot(h.astype(BF16), w_ple_gate_ref[...]) + b_ple_gate_ref[...]))
    o_ref[...] = h + _rms(e, g_post_ple_ref[...])


def _merge_ffn(x2, ya, yb, p2, params):
    tokens = x2.shape[0]
    tm = TOKEN_TILE
    tok_spec = pl.BlockSpec((tm, D_MODEL), lambda t: (t, 0))
    return pl.pallas_call(
        _merge_ffn_kernel,
        grid=(tokens // tm,),
        in_specs=[tok_spec, tok_spec, tok_spec, pl.BlockSpec((tm, PLE_DIM), lambda t: (t, 0))]
                 + [_resident(a.shape) for a in params],
        out_specs=tok_spec,
        out_shape=jax.ShapeDtypeStruct((tokens, D_MODEL), F32),
        compiler_params=pltpu.CompilerParams(
            dimension_semantics=("arbitrary",), vmem_limit_bytes=V7X_VMEM_LIMIT_BYTES),
        name="merge_ffn",
    )(x2, ya, yb, p2, *params)


def _rope_tables(seq):
    pos = jnp.arange(seq, dtype=F32)
    inv = 1.0 / (ROPE_THETA ** (jnp.arange(0, HEAD_DIM, 2, dtype=F32) / HEAD_DIM))
    ang = pos[:, None] * inv[None, :]
    cos = jnp.cos(ang)
    sin = jnp.sin(ang)
    cos_full = jnp.concatenate([cos, cos, cos, cos], axis=-1)
    sin_signed = jnp.concatenate([-sin, sin, -sin, sin], axis=-1)
    return cos_full, sin_signed


def kernel(x, p, norm_pre_mix, w_in, ln_v_gain, ln_v_bias, w_spatial, b_spatial, lambda_q1, lambda_k1, lambda_q2, lambda_k2, subln_gain, w_branch_a, w_branch_b, w_out, norm_post_mix, norm_pre_ffn, w_ff1, w_ff2, norm_post_ffn, w_ple_proj, w_ple_gate, b_ple_gate, norm_post_ple):
    batch, seq, d_model = x.shape
    depth = w_in.shape[0]
    assert d_model == D_MODEL and depth == 1
    assert seq % TOKEN_TILE == 0 and TOKEN_TILE % GMLP_BLOCK == 0
    tokens = batch * seq
    nt = seq // TOKEN_TILE
    x2 = x.reshape(tokens, D_MODEL)
    cos, sin_signed = _rope_tables(seq)

    def row(a):
        return a[0].reshape(1, -1).astype(F32)

    w_in_bf = w_in[0].astype(BF16)
    b_sp_full = jnp.repeat(b_spatial[0].T.astype(F32), D_MODEL // GMLP_GROUPS, axis=1)

    ya, qt, k, vt = _mix_in(
        x2, row(norm_pre_mix), w_in_bf[:, :MIX_WIDTH], row(ln_v_gain), row(ln_v_bias),
        w_spatial[0].astype(F32), b_sp_full, cos, sin_signed, seq)

    lam_params = jnp.concatenate([lambda_q1, lambda_k1, lambda_q2, lambda_k2], axis=0).astype(F32)
    yb = _diff_attn(qt.reshape(batch, nt, D_MODEL, TOKEN_TILE), k.reshape(batch, seq, D_MODEL),
                    vt.reshape(batch, nt, D_MODEL, TOKEN_TILE), lam_params, row(subln_gain))

    params = (
        row(norm_pre_mix), w_in_bf[:, MIX_WIDTH:], w_branch_a[0].astype(BF16),
        w_branch_b[0].astype(BF16), w_out[0].astype(BF16), row(norm_post_mix),
        row(norm_pre_ffn), w_ff1[0].astype(BF16), w_ff2[0].astype(BF16), row(norm_post_ffn),
        w_ple_proj[0].astype(BF16), w_ple_gate[0].astype(BF16), row(b_ple_gate), row(norm_post_ple),
    )
    out = _merge_ffn(x2, ya, yb.reshape(tokens, D_MODEL), p[0].reshape(tokens, PLE_DIM), params)
    return out.reshape(batch, seq, D_MODEL)
```

```python
import math

import jax
import jax.numpy as jnp
from jax import lax
from jax.experimental import pallas as pl
from jax.experimental.pallas import tpu as pltpu

D_MODEL = 1024
CHUNK = 64
GMLP_BLOCK = 128
GMLP_GROUPS = 8
HEAD_DIM = 64
HEADS = 8
HEAD_WIDTH = 2 * HEAD_DIM
FF_WIDTH = 4 * D_MODEL
PLE_DIM = 256
ROPE_THETA = 10000.0
EPS = 1e-6
LAMBDA_INIT = 0.8 - 0.6 * math.exp(-0.3 * 0)

MIX_WIDTH = 5 * D_MODEL
V7X_VMEM_LIMIT_BYTES = 56 * 1024 * 1024
BF16_SUBLANES = 16

TOKEN_TILE = 256
ATTN_TILE = 512

F32 = jnp.float32
BF16 = jnp.bfloat16


def _rms(x, gain):
    return x * lax.rsqrt(jnp.mean(x * x, axis=-1, keepdims=True) + EPS) * gain


def _dot(a, b):
    return jnp.dot(a, b, preferred_element_type=F32)


def _resident(shape):
    zeros = (0,) * len(shape)
    return pl.BlockSpec(shape, lambda *_: zeros, pipeline_mode=pl.Buffered(1))


def _mix_in_kernel(x_ref, g_ref, w_ref, lng_ref, lnb_ref, wsp_ref, bsp_ref, cos_ref, sin_ref,
                   ya_ref, qt_ref, k_ref, vt_ref):
    tm = x_ref.shape[0]
    n = _rms(x_ref[...], g_ref[...]).astype(BF16)

    va = _dot(n, w_ref[:, 4 * D_MODEL:5 * D_MODEL])
    for h in range(HEADS):
        hs = slice(h * HEAD_WIDTH, (h + 1) * HEAD_WIDTH)
        vt_ref[hs, :] = va[:, hs].T.astype(BF16)

    cos = cos_ref[...]
    sin = sin_ref[...]
    lane = lax.broadcasted_iota(jnp.int32, (tm, HEAD_WIDTH), 1)
    low_half = (lane % HEAD_DIM) < (HEAD_DIM // 2)

    def rope(t, h):
        th = t[:, h * HEAD_WIDTH:(h + 1) * HEAD_WIDTH]
        rot = jnp.where(low_half,
                        pltpu.roll(th, HEAD_WIDTH - HEAD_DIM // 2, 1),
                        pltpu.roll(th, HEAD_DIM // 2, 1))
        return th * cos + rot * sin

    t = _dot(n, w_ref[:, 2 * D_MODEL:3 * D_MODEL])
    for h in range(HEADS):
        hs = slice(h * HEAD_WIDTH, (h + 1) * HEAD_WIDTH)
        qt_ref[hs, :] = (rope(t, h) * HEAD_DIM ** -0.5).T.astype(BF16)
    t = _dot(n, w_ref[:, 3 * D_MODEL:4 * D_MODEL])
    for h in range(HEADS):
        hs = slice(h * HEAD_WIDTH, (h + 1) * HEAD_WIDTH)
        k_ref[:, hs] = rope(t, h).astype(BF16)

    gu = jax.nn.gelu(_dot(n, w_ref[:, 0:D_MODEL]))
    gv = jax.nn.gelu(_dot(n, w_ref[:, D_MODEL:2 * D_MODEL]))
    mu = jnp.mean(gv, axis=-1, keepdims=True)
    cen = gv - mu
    var = jnp.mean(cen * cen, axis=-1, keepdims=True)
    vn = (cen * lax.rsqrt(var + EPS) * lng_ref[...] + lnb_ref[...]).astype(BF16)

    row = lax.broadcasted_iota(jnp.int32, (GMLP_BLOCK, GMLP_BLOCK), 0)
    col = lax.broadcasted_iota(jnp.int32, (GMLP_BLOCK, GMLP_BLOCK), 1)
    allowed = (col // CHUNK) <= (row // CHUNK)
    gw = D_MODEL // GMLP_GROUPS
    for g in range(GMLP_GROUPS):
        w_g = jnp.where(allowed, wsp_ref[g], 0.0).astype(BF16)
        cs = slice(g * gw, (g + 1) * gw)
        for blk in range(tm // GMLP_BLOCK):
            rs = slice(blk * GMLP_BLOCK, (blk + 1) * GMLP_BLOCK)
            mixed = _dot(w_g, vn[rs, cs]) + bsp_ref[:, cs]
            ya_ref[rs, cs] = (gu[rs, cs] * mixed).astype(BF16)


def _mix_in(x2, g_pre, w_mix, ln_g, ln_b, w_sp, b_sp_full, cos, sin_signed, seq):
    tokens = x2.shape[0]
    tm = TOKEN_TILE
    tiles = tokens // tm
    tiles_per_seq = seq // tm
    tok_spec = pl.BlockSpec((tm, D_MODEL), lambda t: (t, 0))
    pos_spec = pl.BlockSpec((tm, HEAD_WIDTH), lambda t: (t % tiles_per_seq, 0))
    feat_spec = pl.BlockSpec((None, D_MODEL, tm), lambda t: (t, 0, 0))
    tok_out = jax.ShapeDtypeStruct((tokens, D_MODEL), BF16)
    feat_out = jax.ShapeDtypeStruct((tiles, D_MODEL, tm), BF16)
    return pl.pallas_call(
        _mix_in_kernel,
        grid=(tiles,),
        in_specs=[
            tok_spec,
            _resident((1, D_MODEL)),
            _resident((D_MODEL, MIX_WIDTH)),
            _resident((1, D_MODEL)),
            _resident((1, D_MODEL)),
            _resident((GMLP_GROUPS, GMLP_BLOCK, GMLP_BLOCK)),
            _resident((GMLP_BLOCK, D_MODEL)),
            pos_spec,
            pos_spec,
        ],
        out_specs=[tok_spec, feat_spec, tok_spec, feat_spec],
        out_shape=[tok_out, feat_out, tok_out, feat_out],
        compiler_params=pltpu.CompilerParams(
            dimension_semantics=("arbitrary",), vmem_limit_bytes=V7X_VMEM_LIMIT_BYTES),
        name="mix_in",
    )(x2, g_pre, w_mix, ln_g, ln_b, w_sp, b_sp_full, cos, sin_signed)


def _diff_attn_kernel(qt_ref, k_ref, vt_ref, lam_ref, g_ref, o_ref, s_ref, p_ref, acc_ref):
    slabs = qt_ref.shape[0]
    t = slabs * qt_ref.shape[2]
    i = pl.program_id(2)

    qt = jnp.concatenate([qt_ref[r] for r in range(slabs)], axis=1)
    feat = lax.broadcasted_iota(jnp.int32, (HEAD_WIDTH, t), 0)
    zero = jnp.zeros_like(qt)
    qt2 = jnp.concatenate([jnp.where(feat < HEAD_DIM, qt, zero),
                           jnp.where(feat >= HEAD_DIM, qt, zero)], axis=1)
    ones = jnp.ones((BF16_SUBLANES, t), BF16)

    def qk(j):
        s = _dot(k_ref[pl.ds(pl.multiple_of(j * t, t), t), :], qt2)
        s_ref[...] = s
        return jnp.max(s, axis=0, keepdims=True)

    def softmax_block(s_max, m):
        m_new = jnp.maximum(m, s_max)
        p_ref[...] = jnp.exp(s_ref[...] - m_new).astype(BF16)
        return m_new, jnp.exp(m - m_new)

    def pv(j, alpha):
        vt = jnp.concatenate([vt_ref[slabs * j + r] for r in range(slabs)], axis=1)
        vb = jnp.concatenate([vt, ones], axis=0)
        acc_ref[...] = alpha * acc_ref[...] + _dot(vb, p_ref[...])

    key = lax.broadcasted_iota(jnp.int32, (t, 2 * t), 0)
    qry = lax.broadcasted_iota(jnp.int32, (t, 2 * t), 1)
    allowed = (key // CHUNK) <= ((qry % t) // CHUNK)
    qk(i)
    s_diag = jnp.where(allowed, s_ref[...], -jnp.inf)
    m = jnp.max(s_diag, axis=0, keepdims=True)
    p_ref[...] = jnp.exp(s_diag - m).astype(BF16)
    acc_ref[...] = jnp.zeros_like(acc_ref)

    def kv_step(j, carry):
        s_max, alpha, m, jp = carry
        pv(jp, alpha)
        m, alpha = softmax_block(s_max, m)
        s_max = qk(jnp.minimum(j + 1, i - 1))
        return s_max, alpha, m, j

    _, alpha, _, jp = lax.fori_loop(0, i, kv_step, (qk(0), jnp.ones((1, 2 * t), F32), m, i))
    pv(jp, alpha)
    acc1 = acc_ref[:, :t]
    acc2 = acc_ref[:, t:]

    lam_p = lam_ref[...]
    lam = (jnp.exp(jnp.sum(lam_p[0:1] * lam_p[1:2], axis=-1, keepdims=True))
           - jnp.exp(jnp.sum(lam_p[2:3] * lam_p[3:4], axis=-1, keepdims=True))
           + LAMBDA_INIT)
    o = (acc1[:HEAD_WIDTH] / acc1[HEAD_WIDTH:HEAD_WIDTH + 1]
         - lam * (acc2[:HEAD_WIDTH] / acc2[HEAD_WIDTH:HEAD_WIDTH + 1]))
    y = o * lax.rsqrt(jnp.mean(o * o, axis=0, keepdims=True) + EPS)
    o_ref[...] = (y.T * g_ref[...] * (1.0 - LAMBDA_INIT)).astype(BF16)


def _diff_attn(qt, k, vt, lam_params, subln_gain):
    batch, seq, _ = k.shape
    t = ATTN_TILE
    slabs = t // TOKEN_TILE
    return pl.pallas_call(
        _diff_attn_kernel,
        grid=(batch, HEADS, seq // t),
        in_specs=[
            pl.BlockSpec((None, slabs, HEAD_WIDTH, TOKEN_TILE), lambda b, h, i: (b, i, h, 0)),
            pl.BlockSpec((None, seq, HEAD_WIDTH), lambda b, h, i: (b, 0, h)),
            pl.BlockSpec((None, seq // TOKEN_TILE, HEAD_WIDTH, TOKEN_TILE),
                         lambda b, h, i: (b, 0, h, 0)),
            pl.BlockSpec((4, HEAD_DIM), lambda b, h, i: (0, 0)),
            pl.BlockSpec((1, HEAD_WIDTH), lambda b, h, i: (0, 0)),
        ],
        out_specs=pl.BlockSpec((None, t, HEAD_WIDTH), lambda b, h, i: (b, i, h)),
        out_shape=jax.ShapeDtypeStruct((batch, seq, D_MODEL), BF16),
        scratch_shapes=[
            pltpu.VMEM((t, 2 * t), F32),
            pltpu.VMEM((t, 2 * t), BF16),
            pltpu.VMEM((HEAD_WIDTH + BF16_SUBLANES, 2 * t), F32),
        ],
        compiler_params=pltpu.CompilerParams(
            dimension_semantics=("arbitrary", "arbitrary", "arbitrary"),
            vmem_limit_bytes=V7X_VMEM_LIMIT_BYTES),
        name="diff_attn",
    )(qt, k, vt, lam_params, subln_gain)


def _merge_ffn_kernel(x_ref, ya_ref, yb_ref, p_ref,
                      g_pre_ref, w_gate_ref, w_a_ref, w_b_ref, w_out_ref, g_post_mix_ref,
                      g_pre_ffn_ref, w_ff1_ref, w_ff2_ref, g_post_ffn_ref,
                      w_ple_proj_ref, w_ple_gate_ref, b_ple_gate_ref, g_post_ple_ref,
                      o_ref):
    x = x_ref[...]
    n = _rms(x, g_pre_ref[...]).astype(BF16)
    gates = jax.nn.sigmoid(_dot(n, w_gate_ref[...]))
    merged = (gates[:, :D_MODEL] * _dot(ya_ref[...], w_a_ref[...])
              + gates[:, D_MODEL:] * _dot(yb_ref[...], w_b_ref[...]))
    h = x + _rms(_dot(merged.astype(BF16), w_out_ref[...]), g_post_mix_ref[...])

    f = _dot(_rms(h, g_pre_ffn_ref[...]).astype(BF16), w_ff1_ref[...])
    f = jnp.square(jnp.maximum(f, 0.0))
    h = h + _rms(_dot(f.astype(BF16), w_ff2_ref[...]), g_post_ffn_ref[...])

    e = (_dot(p_ref[...].astype(BF16), w_ple_proj_ref[...])
         * jax.nn.sigmoid(_dot(h.astype(BF16), w_ple_gate_ref[...]) + b_ple_gate_ref[...]))
    o_ref[...] = h + _rms(e, g_post_ple_ref[...])


def _merge_ffn(x2, ya, yb, p2, params):
    tokens = x2.shape[0]
    tm = TOKEN_TILE
    tok_spec = pl.BlockSpec((tm, D_MODEL), lambda t: (t, 0))
    return pl.pallas_call(
        _merge_ffn_kernel,
        grid=(tokens // tm,),
        in_specs=[tok_spec, tok_spec, tok_spec, pl.BlockSpec((tm, PLE_DIM), lambda t: (t, 0))]
                 + [_resident(a.shape) for a in params],
        out_specs=tok_spec,
        out_shape=jax.ShapeDtypeStruct((tokens, D_MODEL), F32),
        compiler_params=pltpu.CompilerParams(
            dimension_semantics=("arbitrary",), vmem_limit_bytes=V7X_VMEM_LIMIT_BYTES),
        name="merge_ffn",
    )(x2, ya, yb, p2, *params)


def _rope_tables(seq):
    pos = jnp.arange(seq, dtype=F32)
    inv = 1.0 / (ROPE_THETA ** (jnp.arange(0, HEAD_DIM, 2, dtype=F32) / HEAD_DIM))
    ang = pos[:, None] * inv[None, :]
    cos = jnp.cos(ang)
    sin = jnp.sin(ang)
    cos_full = jnp.concatenate([cos, cos, cos, cos], axis=-1)
    sin_signed = jnp.concatenate([-sin, sin, -sin, sin], axis=-1)
    return cos_full, sin_signed


def kernel(x, p, norm_pre_mix, w_in, ln_v_gain, ln_v_bias, w_spatial, b_spatial, lambda_q1, lambda_k1, lambda_q2, lambda_k2, subln_gain, w_branch_a, w_branch_b, w_out, norm_post_mix, norm_pre_ffn, w_ff1, w_ff2, norm_post_ffn, w_ple_proj, w_ple_gate, b_ple_gate, norm_post_ple):
    batch, seq, d_model = x.shape
    depth = w_in.shape[0]
    assert d_model == D_MODEL and depth == 1
    assert seq % ATTN_TILE == 0 and ATTN_TILE % TOKEN_TILE == 0 and TOKEN_TILE % GMLP_BLOCK == 0
    tokens = batch * seq
    nt = seq // TOKEN_TILE
    x2 = x.reshape(tokens, D_MODEL)
    cos, sin_signed = _rope_tables(seq)

    def row(a):
        return a[0].reshape(1, -1).astype(F32)

    w_in_bf = w_in[0].astype(BF16)
    b_sp_full = jnp.repeat(b_spatial[0].T.astype(F32), D_MODEL // GMLP_GROUPS, axis=1)

    ya, qt, k, vt = _mix_in(
        x2, row(norm_pre_mix), w_in_bf[:, :MIX_WIDTH], row(ln_v_gain), row(ln_v_bias),
        w_spatial[0].astype(F32), b_sp_full, cos, sin_signed, seq)

    lam_params = jnp.concatenate([lambda_q1, lambda_k1, lambda_q2, lambda_k2], axis=0).astype(F32)
    yb = _diff_attn(qt.reshape(batch, nt, D_MODEL, TOKEN_TILE), k.reshape(batch, seq, D_MODEL),
                    vt.reshape(batch, nt, D_MODEL, TOKEN_TILE), lam_params, row(subln_gain))

    params = (
        row(norm_pre_mix), w_in_bf[:, MIX_WIDTH:], w_branch_a[0].astype(BF16),
        w_branch_b[0].astype(BF16), w_out[0].astype(BF16), row(norm_post_mix),
        row(norm_pre_ffn), w_ff1[0].astype(BF16), w_ff2[0].astype(BF16), row(norm_post_ffn),
        w_ple_proj[0].astype(BF16), w_ple_gate[0].astype(BF16), row(b_ple_gate), row(norm_post_ple),
    )
    out = _merge_ffn(x2, ya, yb.reshape(tokens, D_MODEL), p[0].reshape(tokens, PLE_DIM), params)
    return out.reshape(batch, seq, D_MODEL)
```

```python
import math

import jax
import jax.numpy as jnp
from jax import lax
from jax.experimental import pallas as pl
from jax.experimental.pallas import tpu as pltpu

D_MODEL = 1024
CHUNK = 64
GMLP_BLOCK = 128
GMLP_GROUPS = 8
HEAD_DIM = 64
HEADS = 8
HEAD_WIDTH = 2 * HEAD_DIM
FF_WIDTH = 4 * D_MODEL
PLE_DIM = 256
ROPE_THETA = 10000.0
EPS = 1e-6
LAMBDA_INIT = 0.8 - 0.6 * math.exp(-0.3 * 0)
Q_SCALE = HEAD_DIM ** -0.5 * math.log2(math.e)

MIX_WIDTH = 5 * D_MODEL
V7X_VMEM_LIMIT_BYTES = 56 * 1024 * 1024
BF16_SUBLANES = 16

TOKEN_TILE = 256
ATTN_TILE = 512
ATTN_HEADS = 2

F32 = jnp.float32
BF16 = jnp.bfloat16


def _rms(x, gain):
    return x * lax.rsqrt(jnp.mean(x * x, axis=-1, keepdims=True) + EPS) * gain


def _dot(a, b):
    return jnp.dot(a, b, preferred_element_type=F32)


def _resident(shape):
    zeros = (0,) * len(shape)
    return pl.BlockSpec(shape, lambda *_: zeros, pipeline_mode=pl.Buffered(1))


def _mix_in_kernel(x_ref, g_ref, w_ref, lng_ref, lnb_ref, wsp_ref, bsp_ref, cos_ref, sin_ref,
                   ya_ref, qt_ref, k_ref, vt_ref):
    tm = x_ref.shape[0]
    n = _rms(x_ref[...], g_ref[...]).astype(BF16)

    va = _dot(n, w_ref[:, 4 * D_MODEL:5 * D_MODEL])
    for h in range(HEADS):
        hs = slice(h * HEAD_WIDTH, (h + 1) * HEAD_WIDTH)
        vt_ref[hs, :] = va[:, hs].T.astype(BF16)

    cos = cos_ref[...]
    sin = sin_ref[...]
    lane = lax.broadcasted_iota(jnp.int32, (tm, HEAD_WIDTH), 1)
    low_half = (lane % HEAD_DIM) < (HEAD_DIM // 2)

    def rope(t, h):
        th = t[:, h * HEAD_WIDTH:(h + 1) * HEAD_WIDTH]
        rot = jnp.where(low_half,
                        pltpu.roll(th, HEAD_WIDTH - HEAD_DIM // 2, 1),
                        pltpu.roll(th, HEAD_DIM // 2, 1))
        return th * cos + rot * sin

    t = _dot(n, w_ref[:, 2 * D_MODEL:3 * D_MODEL])
    for h in range(HEADS):
        hs = slice(h * HEAD_WIDTH, (h + 1) * HEAD_WIDTH)
        qt_ref[hs, :] = (rope(t, h) * Q_SCALE).T.astype(BF16)
    t = _dot(n, w_ref[:, 3 * D_MODEL:4 * D_MODEL])
    for h in range(HEADS):
        hs = slice(h * HEAD_WIDTH, (h + 1) * HEAD_WIDTH)
        k_ref[:, hs] = rope(t, h).astype(BF16)

    gu = jax.nn.gelu(_dot(n, w_ref[:, 0:D_MODEL]))
    gv = jax.nn.gelu(_dot(n, w_ref[:, D_MODEL:2 * D_MODEL]))
    mu = jnp.mean(gv, axis=-1, keepdims=True)
    cen = gv - mu
    var = jnp.mean(cen * cen, axis=-1, keepdims=True)
    vn = (cen * lax.rsqrt(var + EPS) * lng_ref[...] + lnb_ref[...]).astype(BF16)

    row = lax.broadcasted_iota(jnp.int32, (GMLP_BLOCK, GMLP_BLOCK), 0)
    col = lax.broadcasted_iota(jnp.int32, (GMLP_BLOCK, GMLP_BLOCK), 1)
    allowed = (col // CHUNK) <= (row // CHUNK)
    gw = D_MODEL // GMLP_GROUPS
    for g in range(GMLP_GROUPS):
        w_g = jnp.where(allowed, wsp_ref[g], 0.0).astype(BF16)
        cs = slice(g * gw, (g + 1) * gw)
        for blk in range(tm // GMLP_BLOCK):
            rs = slice(blk * GMLP_BLOCK, (blk + 1) * GMLP_BLOCK)
            mixed = _dot(w_g, vn[rs, cs]) + bsp_ref[:, cs]
            ya_ref[rs, cs] = (gu[rs, cs] * mixed).astype(BF16)


def _mix_in(x2, g_pre, w_mix, ln_g, ln_b, w_sp, b_sp_full, cos, sin_signed, seq):
    tokens = x2.shape[0]
    tm = TOKEN_TILE
    tiles = tokens // tm
    tiles_per_seq = seq // tm
    tok_spec = pl.BlockSpec((tm, D_MODEL), lambda t: (t, 0))
    pos_spec = pl.BlockSpec((tm, HEAD_WIDTH), lambda t: (t % tiles_per_seq, 0))
    feat_spec = pl.BlockSpec((None, D_MODEL, tm), lambda t: (t, 0, 0))
    tok_out = jax.ShapeDtypeStruct((tokens, D_MODEL), BF16)
    feat_out = jax.ShapeDtypeStruct((tiles, D_MODEL, tm), BF16)
    return pl.pallas_call(
        _mix_in_kernel,
        grid=(tiles,),
        in_specs=[
            tok_spec,
            _resident((1, D_MODEL)),
            _resident((D_MODEL, MIX_WIDTH)),
            _resident((1, D_MODEL)),
            _resident((1, D_MODEL)),
            _resident((GMLP_GROUPS, GMLP_BLOCK, GMLP_BLOCK)),
            _resident((GMLP_BLOCK, D_MODEL)),
            pos_spec,
            pos_spec,
        ],
        out_specs=[tok_spec, feat_spec, tok_spec, feat_spec],
        out_shape=[tok_out, feat_out, tok_out, feat_out],
        compiler_params=pltpu.CompilerParams(
            dimension_semantics=("arbitrary",), vmem_limit_bytes=V7X_VMEM_LIMIT_BYTES),
        name="mix_in",
    )(x2, g_pre, w_mix, ln_g, ln_b, w_sp, b_sp_full, cos, sin_signed)


def _diff_attn_kernel(qt_ref, k_ref, vt_ref, lam_ref, g_ref, o_ref, s_ref, p_ref, acc_ref):
    slabs = qt_ref.shape[0]
    t = slabs * qt_ref.shape[2]
    heads = qt_ref.shape[1] // HEAD_WIDTH
    w = 2 * t
    i = pl.program_id(2)

    def head_rows(g):
        return slice(g * HEAD_WIDTH, (g + 1) * HEAD_WIDTH)

    def head_lanes(g):
        return slice(g * w, (g + 1) * w)

    feat = lax.broadcasted_iota(jnp.int32, (HEAD_WIDTH, t), 0)
    qt2 = []
    for g in range(heads):
        qt = jnp.concatenate([qt_ref[r, head_rows(g), :] for r in range(slabs)], axis=1)
        zero = jnp.zeros_like(qt)
        qt2.append(jnp.concatenate([jnp.where(feat < HEAD_DIM, qt, zero),
                                    jnp.where(feat >= HEAD_DIM, qt, zero)], axis=1))
    ones = jnp.ones((BF16_SUBLANES, t), BF16)

    def qk(j):
        rows = pl.ds(pl.multiple_of(j * t, t), t)
        maxes = []
        for g in range(heads):
            s = _dot(k_ref[rows, head_rows(g)], qt2[g])
            s_ref[:, head_lanes(g)] = s
            maxes.append(jnp.max(s, axis=0, keepdims=True))
        return jnp.concatenate(maxes, axis=1)

    def softmax_block(s_max, m):
        m_new = jnp.maximum(m, s_max)
        p_ref[...] = jnp.exp2(s_ref[...] - m_new).astype(BF16)
        return m_new, jnp.exp2(m - m_new)

    def pv(j, alpha):
        for g in range(heads):
            vt = jnp.concatenate(
                [vt_ref[slabs * j + r, head_rows(g), :] for r in range(slabs)], axis=1)
            vb = jnp.concatenate([vt, ones], axis=0)
            cs = head_lanes(g)
            acc_ref[:, cs] = alpha[:, cs] * acc_ref[:, cs] + _dot(vb, p_ref[:, cs])

    s_max = qk(0)
    p_ref[...] = jnp.zeros_like(p_ref)
    acc_ref[...] = jnp.zeros_like(acc_ref)

    def kv_step(j, carry):
        s_max, alpha, m, jp = carry
        pv(jp, alpha)
        m, alpha = softmax_block(s_max, m)
        s_max = qk(j + 1)
        return s_max, alpha, m, j

    lanes = heads * w
    init = (s_max, jnp.ones((1, lanes), F32), jnp.full((1, lanes), -jnp.inf, F32), 0)
    _, alpha, m, jp = lax.fori_loop(0, i, kv_step, init)
    pv(jp, alpha)

    qry = lax.broadcasted_iota(jnp.int32, (1, lanes), 1) % t
    bands = [jnp.where(qry >= c * CHUNK, s_ref[c * CHUNK:(c + 1) * CHUNK, :], -jnp.inf)
             for c in range(t // CHUNK)]
    m_new = m
    for band in bands:
        m_new = jnp.maximum(m_new, jnp.max(band, axis=0, keepdims=True))
    for c, band in enumerate(bands):
        p_ref[c * CHUNK:(c + 1) * CHUNK, :] = jnp.exp2(band - m_new).astype(BF16)
    pv(i, jnp.exp2(m - m_new))

    lam_p = lam_ref[...]
    lam = (jnp.exp(jnp.sum(lam_p[0:1] * lam_p[1:2], axis=-1, keepdims=True))
           - jnp.exp(jnp.sum(lam_p[2:3] * lam_p[3:4], axis=-1, keepdims=True))
           + LAMBDA_INIT)
    inv_l = 1.0 / acc_ref[HEAD_WIDTH:HEAD_WIDTH + 1, :]
    for g in range(heads):
        c1 = slice(g * w, g * w + t)
        c2 = slice(g * w + t, (g + 1) * w)
        o = (acc_ref[:HEAD_WIDTH, c1] * inv_l[:, c1]
             - lam * (acc_ref[:HEAD_WIDTH, c2] * inv_l[:, c2]))
        y = o * lax.rsqrt(jnp.mean(o * o, axis=0, keepdims=True) + EPS)
        o_ref[:, head_rows(g)] = (y.T * g_ref[...] * (1.0 - LAMBDA_INIT)).astype(BF16)


def _diff_attn(qt, k, vt, lam_params, subln_gain):
    batch, seq, _ = k.shape
    t = ATTN_TILE
    slabs = t // TOKEN_TILE
    gw = ATTN_HEADS * HEAD_WIDTH
    lanes = ATTN_HEADS * 2 * t
    return pl.pallas_call(
        _diff_attn_kernel,
        grid=(batch, HEADS // ATTN_HEADS, seq // t),
        in_specs=[
            pl.BlockSpec((None, slabs, gw, TOKEN_TILE), lambda b, h, i: (b, i, h, 0)),
            pl.BlockSpec((None, seq, gw), lambda b, h, i: (b, 0, h)),
            pl.BlockSpec((None, seq // TOKEN_TILE, gw, TOKEN_TILE), lambda b, h, i: (b, 0, h, 0)),
            pl.BlockSpec((4, HEAD_DIM), lambda b, h, i: (0, 0)),
            pl.BlockSpec((1, HEAD_WIDTH), lambda b, h, i: (0, 0)),
        ],
        out_specs=pl.BlockSpec((None, t, gw), lambda b, h, i: (b, i, h)),
        out_shape=jax.ShapeDtypeStruct((batch, seq, D_MODEL), BF16),
        scratch_shapes=[
            pltpu.VMEM((t, lanes), F32),
            pltpu.VMEM((t, lanes), BF16),
            pltpu.VMEM((HEAD_WIDTH + BF16_SUBLANES, lanes), F32),
        ],
        compiler_params=pltpu.CompilerParams(
            dimension_semantics=("arbitrary", "arbitrary", "arbitrary"),
            vmem_limit_bytes=V7X_VMEM_LIMIT_BYTES),
        name="diff_attn",
    )(qt, k, vt, lam_params, subln_gain)


def _merge_ffn_kernel(x_ref, ya_ref, yb_ref, p_ref,
                      g_pre_ref, w_gate_ref, w_a_ref, w_b_ref, w_out_ref, g_post_mix_ref,
                      g_pre_ffn_ref, w_ff1_ref, w_ff2_ref, g_post_ffn_ref,
                      w_ple_proj_ref, w_ple_gate_ref, b_ple_gate_ref, g_post_ple_ref,
                      o_ref):
    x = x_ref[...]
    n = _rms(x, g_pre_ref[...]).astype(BF16)
    gates = jax.nn.sigmoid(_dot(n, w_gate_ref[...]))
    merged = (gates[:, :D_MODEL] * _dot(ya_ref[...], w_a_ref[...])
              + gates[:, D_MODEL:] * _dot(yb_ref[...], w_b_ref[...]))
    h = x + _rms(_dot(merged.astype(BF16), w_out_ref[...]), g_post_mix_ref[...])

    f = _dot(_rms(h, g_pre_ffn_ref[...]).astype(BF16), w_ff1_ref[...])
    f = jnp.square(jnp.maximum(f, 0.0))
    h = h + _rms(_dot(f.astype(BF16), w_ff2_ref[...]), g_post_ffn_ref[...])

    e = (_dot(p_ref[...].astype(BF16), w_ple_proj_ref[...])
         * jax.nn.sigmoid(_dot(h.astype(BF16), w_ple_gate_ref[...]) + b_ple_gate_ref[...]))
    o_ref[...] = h + _rms(e, g_post_ple_ref[...])


def _merge_ffn(x2, ya, yb, p2, params):
    tokens = x2.shape[0]
    tm = TOKEN_TILE
    tok_spec = pl.BlockSpec((tm, D_MODEL), lambda t: (t, 0))
    return pl.pallas_call(
        _merge_ffn_kernel,
        grid=(tokens // tm,),
        in_specs=[tok_spec, tok_spec, tok_spec, pl.BlockSpec((tm, PLE_DIM), lambda t: (t, 0))]
                 + [_resident(a.shape) for a in params],
        out_specs=tok_spec,
        out_shape=jax.ShapeDtypeStruct((tokens, D_MODEL), F32),
        compiler_params=pltpu.CompilerParams(
            dimension_semantics=("arbitrary",), vmem_limit_bytes=V7X_VMEM_LIMIT_BYTES),
        name="merge_ffn",
    )(x2, ya, yb, p2, *params)


def _rope_tables(seq):
    pos = jnp.arange(seq, dtype=F32)
    inv = 1.0 / (ROPE_THETA ** (jnp.arange(0, HEAD_DIM, 2, dtype=F32) / HEAD_DIM))
    ang = pos[:, None] * inv[None, :]
    cos = jnp.cos(ang)
    sin = jnp.sin(ang)
    cos_full = jnp.concatenate([cos, cos, cos, cos], axis=-1)
    sin_signed = jnp.concatenate([-sin, sin, -sin, sin], axis=-1)
    return cos_full, sin_signed


def kernel(x, p, norm_pre_mix, w_in, ln_v_gain, ln_v_bias, w_spatial, b_spatial, lambda_q1, lambda_k1, lambda_q2, lambda_k2, subln_gain, w_branch_a, w_branch_b, w_out, norm_post_mix, norm_pre_ffn, w_ff1, w_ff2, norm_post_ffn, w_ple_proj, w_ple_gate, b_ple_gate, norm_post_ple):
    batch, seq, d_model = x.shape
    depth = w_in.shape[0]
    assert d_model == D_MODEL and depth == 1
    assert seq % ATTN_TILE == 0 and ATTN_TILE % TOKEN_TILE == 0 and TOKEN_TILE % GMLP_BLOCK == 0
    tokens = batch * seq
    nt = seq // TOKEN_TILE
    x2 = x.reshape(tokens, D_MODEL)
    cos, sin_signed = _rope_tables(seq)

    def row(a):
        return a[0].reshape(1, -1).astype(F32)

    w_in_bf = w_in[0].astype(BF16)
    b_sp_full = jnp.repeat(b_spatial[0].T.astype(F32), D_MODEL // GMLP_GROUPS, axis=1)

    ya, qt, k, vt = _mix_in(
        x2, row(norm_pre_mix), w_in_bf[:, :MIX_WIDTH], row(ln_v_gain), row(ln_v_bias),
        w_spatial[0].astype(F32), b_sp_full, cos, sin_signed, seq)

    lam_params = jnp.concatenate([lambda_q1, lambda_k1, lambda_q2, lambda_k2], axis=0).astype(F32)
    yb = _diff_attn(qt.reshape(batch, nt, D_MODEL, TOKEN_TILE), k.reshape(batch, seq, D_MODEL),
                    vt.reshape(batch, nt, D_MODEL, TOKEN_TILE), lam_params, row(subln_gain))

    params = (
        row(norm_pre_mix), w_in_bf[:, MIX_WIDTH:], w_branch_a[0].astype(BF16),
        w_branch_b[0].astype(BF16), w_out[0].astype(BF16), row(norm_post_mix),
        row(norm_pre_ffn), w_ff1[0].astype(BF16), w_ff2[0].astype(BF16), row(norm_post_ffn),
        w_ple_proj[0].astype(BF16), w_ple_gate[0].astype(BF16), row(b_ple_gate), row(norm_post_ple),
    )
    out = _merge_ffn(x2, ya, yb.reshape(tokens, D_MODEL), p[0].reshape(tokens, PLE_DIM), params)
    return out.reshape(batch, seq, D_MODEL)
```

```python
import math

import jax
import jax.numpy as jnp
from jax import lax
from jax.experimental import pallas as pl
from jax.experimental.pallas import tpu as pltpu

D_MODEL = 1024
CHUNK = 64
GMLP_BLOCK = 128
GMLP_GROUPS = 8
HEAD_DIM = 64
HEADS = 8
HEAD_WIDTH = 2 * HEAD_DIM
FF_WIDTH = 4 * D_MODEL
PLE_DIM = 256
ROPE_THETA = 10000.0
EPS = 1e-6
LAMBDA_INIT = 0.8 - 0.6 * math.exp(-0.3 * 0)
Q_SCALE = HEAD_DIM ** -0.5 * math.log2(math.e)

MIX_WIDTH = 5 * D_MODEL
V7X_VMEM_LIMIT_BYTES = 56 * 1024 * 1024
BF16_SUBLANES = 16

TOKEN_TILE = 256
ATTN_TILE = 512
ATTN_HEADS = 4

F32 = jnp.float32
BF16 = jnp.bfloat16


def _rms(x, gain):
    return x * lax.rsqrt(jnp.mean(x * x, axis=-1, keepdims=True) + EPS) * gain


def _dot(a, b):
    return jnp.dot(a, b, preferred_element_type=F32)


def _resident(shape):
    zeros = (0,) * len(shape)
    return pl.BlockSpec(shape, lambda *_: zeros, pipeline_mode=pl.Buffered(1))


def _mix_in_kernel(x_ref, g_ref, w_ref, lng_ref, lnb_ref, wsp_ref, bsp_ref, cos_ref, sin_ref,
                   ya_ref, qt_ref, k_ref, vt_ref):
    tm = x_ref.shape[0]
    n = _rms(x_ref[...], g_ref[...]).astype(BF16)

    va = _dot(n, w_ref[:, 4 * D_MODEL:5 * D_MODEL])
    for h in range(HEADS):
        hs = slice(h * HEAD_WIDTH, (h + 1) * HEAD_WIDTH)
        vt_ref[hs, :] = va[:, hs].T.astype(BF16)

    cos = cos_ref[...]
    sin = sin_ref[...]
    lane = lax.broadcasted_iota(jnp.int32, (tm, HEAD_WIDTH), 1)
    low_half = (lane % HEAD_DIM) < (HEAD_DIM // 2)

    def rope(t, h):
        th = t[:, h * HEAD_WIDTH:(h + 1) * HEAD_WIDTH]
        rot = jnp.where(low_half,
                        pltpu.roll(th, HEAD_WIDTH - HEAD_DIM // 2, 1),
                        pltpu.roll(th, HEAD_DIM // 2, 1))
        return th * cos + rot * sin

    t = _dot(n, w_ref[:, 2 * D_MODEL:3 * D_MODEL])
    for h in range(HEADS):
        hs = slice(h * HEAD_WIDTH, (h + 1) * HEAD_WIDTH)
        qt_ref[hs, :] = (rope(t, h) * Q_SCALE).T.astype(BF16)
    t = _dot(n, w_ref[:, 3 * D_MODEL:4 * D_MODEL])
    for h in range(HEADS):
        hs = slice(h * HEAD_WIDTH, (h + 1) * HEAD_WIDTH)
        k_ref[:, hs] = rope(t, h).astype(BF16)

    gu = jax.nn.gelu(_dot(n, w_ref[:, 0:D_MODEL]))
    gv = jax.nn.gelu(_dot(n, w_ref[:, D_MODEL:2 * D_MODEL]))
    mu = jnp.mean(gv, axis=-1, keepdims=True)
    cen = gv - mu
    var = jnp.mean(cen * cen, axis=-1, keepdims=True)
    vn = (cen * lax.rsqrt(var + EPS) * lng_ref[...] + lnb_ref[...]).astype(BF16)

    row = lax.broadcasted_iota(jnp.int32, (GMLP_BLOCK, GMLP_BLOCK), 0)
    col = lax.broadcasted_iota(jnp.int32, (GMLP_BLOCK, GMLP_BLOCK), 1)
    allowed = (col // CHUNK) <= (row // CHUNK)
    gw = D_MODEL // GMLP_GROUPS
    for g in range(GMLP_GROUPS):
        w_g = jnp.where(allowed, wsp_ref[g], 0.0).astype(BF16)
        cs = slice(g * gw, (g + 1) * gw)
        for blk in range(tm // GMLP_BLOCK):
            rs = slice(blk * GMLP_BLOCK, (blk + 1) * GMLP_BLOCK)
            mixed = _dot(w_g, vn[rs, cs]) + bsp_ref[:, cs]
            ya_ref[rs, cs] = (gu[rs, cs] * mixed).astype(BF16)


def _mix_in(x2, g_pre, w_mix, ln_g, ln_b, w_sp, b_sp_full, cos, sin_signed, seq):
    tokens = x2.shape[0]
    tm = TOKEN_TILE
    tiles = tokens // tm
    tiles_per_seq = seq // tm
    tok_spec = pl.BlockSpec((tm, D_MODEL), lambda t: (t, 0))
    pos_spec = pl.BlockSpec((tm, HEAD_WIDTH), lambda t: (t % tiles_per_seq, 0))
    feat_spec = pl.BlockSpec((None, D_MODEL, tm), lambda t: (t, 0, 0))
    tok_out = jax.ShapeDtypeStruct((tokens, D_MODEL), BF16)
    feat_out = jax.ShapeDtypeStruct((tiles, D_MODEL, tm), BF16)
    return pl.pallas_call(
        _mix_in_kernel,
        grid=(tiles,),
        in_specs=[
            tok_spec,
            _resident((1, D_MODEL)),
            _resident((D_MODEL, MIX_WIDTH)),
            _resident((1, D_MODEL)),
            _resident((1, D_MODEL)),
            _resident((GMLP_GROUPS, GMLP_BLOCK, GMLP_BLOCK)),
            _resident((GMLP_BLOCK, D_MODEL)),
            pos_spec,
            pos_spec,
        ],
        out_specs=[tok_spec, feat_spec, tok_spec, feat_spec],
        out_shape=[tok_out, feat_out, tok_out, feat_out],
        compiler_params=pltpu.CompilerParams(
            dimension_semantics=("arbitrary",), vmem_limit_bytes=V7X_VMEM_LIMIT_BYTES),
        name="mix_in",
    )(x2, g_pre, w_mix, ln_g, ln_b, w_sp, b_sp_full, cos, sin_signed)


def _diff_attn_kernel(qt_ref, k_ref, vt_ref, lam_ref, g_ref, o_ref, *head_scratch):
    t = o_ref.shape[0]
    slabs = qt_ref.shape[0]
    heads = qt_ref.shape[1] // HEAD_WIDTH
    w = 2 * t
    s_refs = head_scratch[0::3]
    p_refs = head_scratch[1::3]
    acc_refs = head_scratch[2::3]
    i = pl.program_id(2)
    every_head = range(heads)

    def head_rows(g):
        return slice(g * HEAD_WIDTH, (g + 1) * HEAD_WIDTH)

    def head_lanes(g):
        return slice(g * w, (g + 1) * w)

    feat = lax.broadcasted_iota(jnp.int32, (HEAD_WIDTH, t), 0)

    def query_operand(g):
        qt = jnp.concatenate([qt_ref[r, head_rows(g), :] for r in range(slabs)], axis=1)
        zero = jnp.zeros_like(qt)
        return jnp.concatenate([jnp.where(feat < HEAD_DIM, qt, zero),
                                jnp.where(feat >= HEAD_DIM, qt, zero)], axis=1)

    ones = jnp.ones((BF16_SUBLANES, t), BF16)

    def qk(g, qt2, j):
        s = _dot(k_ref[pl.ds(pl.multiple_of(j * t, t), t), head_rows(g)], qt2)
        s_refs[g][...] = s
        return jnp.max(s, axis=0, keepdims=True)

    def softmax_block(g, s_max, m):
        m_new = jnp.maximum(m, s_max)
        p_refs[g][...] = jnp.exp2(s_refs[g][...] - m_new).astype(BF16)
        return m_new, jnp.exp2(m - m_new)

    def pv(g, j, alpha):
        vt = jnp.concatenate(
            [vt_ref[slabs * j + r, head_rows(g), :] for r in range(slabs)], axis=1)
        vb = jnp.concatenate([vt, ones], axis=0)
        acc_refs[g][...] = alpha * acc_refs[g][...] + _dot(vb, p_refs[g][...])

    qt2 = [query_operand(g) for g in every_head]

    s_max0 = [qk(g, qt2[g], 0) for g in every_head]
    for g in every_head:
        p_refs[g][...] = jnp.zeros_like(p_refs[g])
        acc_refs[g][...] = jnp.zeros_like(acc_refs[g])

    def kv_step(j, carry):
        s_max, alpha, m, jp = carry
        for g in every_head:
            pv(g, jp, alpha[g])
        stats = [softmax_block(g, s_max[g], m[g]) for g in every_head]
        s_max = [qk(g, qt2[g], j + 1) for g in every_head]
        return s_max, [a for _, a in stats], [mm for mm, _ in stats], j

    init = (s_max0,
            [jnp.ones((1, w), F32) for _ in every_head],
            [jnp.full((1, w), -jnp.inf, F32) for _ in every_head],
            0)
    _, alpha, m, jp = lax.fori_loop(0, i, kv_step, init)
    for g in every_head:
        pv(g, jp, alpha[g])

    qry = lax.broadcasted_iota(jnp.int32, (1, w), 1) % t
    lam_p = lam_ref[...]
    lam = (jnp.exp(jnp.sum(lam_p[0:1] * lam_p[1:2], axis=-1, keepdims=True))
           - jnp.exp(jnp.sum(lam_p[2:3] * lam_p[3:4], axis=-1, keepdims=True))
           + LAMBDA_INIT)
    for g in every_head:
        def band(c, g=g):
            return jnp.where(qry >= c * CHUNK, s_refs[g][c * CHUNK:(c + 1) * CHUNK, :], -jnp.inf)

        m_new = m[g]
        for c in range(t // CHUNK):
            m_new = jnp.maximum(m_new, jnp.max(band(c), axis=0, keepdims=True))
        for c in range(t // CHUNK):
            p_refs[g][c * CHUNK:(c + 1) * CHUNK, :] = jnp.exp2(band(c) - m_new).astype(BF16)
        pv(g, i, jnp.exp2(m[g] - m_new))
        inv_l = 1.0 / acc_refs[g][HEAD_WIDTH:HEAD_WIDTH + 1, :]
        o = (acc_refs[g][:HEAD_WIDTH, :t] * inv_l[:, :t]
             - lam * (acc_refs[g][:HEAD_WIDTH, t:] * inv_l[:, t:]))
        y = o * lax.rsqrt(jnp.mean(o * o, axis=0, keepdims=True) + EPS)
        o_ref[:, head_rows(g)] = (y.T * g_ref[...] * (1.0 - LAMBDA_INIT)).astype(BF16)


def _diff_attn(qt, k, vt, lam_params, subln_gain):
    batch, seq, _ = k.shape
    t = ATTN_TILE
    gw = ATTN_HEADS * HEAD_WIDTH
    w = 2 * t
    slabs = t // TOKEN_TILE
    head_scratch = [
        pltpu.VMEM((t, w), F32),
        pltpu.VMEM((t, w), BF16),
        pltpu.VMEM((HEAD_WIDTH + BF16_SUBLANES, w), F32),
    ]
    return pl.pallas_call(
        _diff_attn_kernel,
        grid=(batch, HEADS // ATTN_HEADS, seq // t),
        in_specs=[
            pl.BlockSpec((None, slabs, gw, TOKEN_TILE), lambda b, h, i: (b, i, h, 0)),
            pl.BlockSpec((None, seq, gw), lambda b, h, i: (b, 0, h)),
            pl.BlockSpec((None, seq // TOKEN_TILE, gw, TOKEN_TILE), lambda b, h, i: (b, 0, h, 0)),
            pl.BlockSpec((4, HEAD_DIM), lambda b, h, i: (0, 0)),
            pl.BlockSpec((1, HEAD_WIDTH), lambda b, h, i: (0, 0)),
        ],
        out_specs=pl.BlockSpec((None, t, gw), lambda b, h, i: (b, i, h)),
        out_shape=jax.ShapeDtypeStruct((batch, seq, D_MODEL), BF16),
        scratch_shapes=head_scratch * ATTN_HEADS,
        compiler_params=pltpu.CompilerParams(
            dimension_semantics=("arbitrary", "arbitrary", "arbitrary"),
            vmem_limit_bytes=V7X_VMEM_LIMIT_BYTES),
        name="diff_attn",
    )(qt, k, vt, lam_params, subln_gain)


def _merge_ffn_kernel(x_ref, ya_ref, yb_ref, p_ref,
                      g_pre_ref, w_gate_ref, w_a_ref, w_b_ref, w_out_ref, g_post_mix_ref,
                      g_pre_ffn_ref, w_ff1_ref, w_ff2_ref, g_post_ffn_ref,
                      w_ple_proj_ref, w_ple_gate_ref, b_ple_gate_ref, g_post_ple_ref,
                      o_ref):
    x = x_ref[...]
    n = _rms(x, g_pre_ref[...]).astype(BF16)
    gates = jax.nn.sigmoid(_dot(n, w_gate_ref[...]))
    merged = (gates[:, :D_MODEL] * _dot(ya_ref[...], w_a_ref[...])
              + gates[:, D_MODEL:] * _dot(yb_ref[...], w_b_ref[...]))
    h = x + _rms(_dot(merged.astype(BF16), w_out_ref[...]), g_post_mix_ref[...])

    f = _dot(_rms(h, g_pre_ffn_ref[...]).astype(BF16), w_ff1_ref[...])
    f = jnp.square(jnp.maximum(f, 0.0))
    h = h + _rms(_dot(f.astype(BF16), w_ff2_ref[...]), g_post_ffn_ref[...])

    e = (_dot(p_ref[...].astype(BF16), w_ple_proj_ref[...])
         * jax.nn.sigmoid(_dot(h.astype(BF16), w_ple_gate_ref[...]) + b_ple_gate_ref[...]))
    o_ref[...] = h + _rms(e, g_post_ple_ref[...])


def _merge_ffn(x2, ya, yb, p2, params):
    tokens = x2.shape[0]
    tm = TOKEN_TILE
    tok_spec = pl.BlockSpec((tm, D_MODEL), lambda t: (t, 0))
    return pl.pallas_call(
        _merge_ffn_kernel,
        grid=(tokens // tm,),
        in_specs=[tok_spec, tok_spec, tok_spec, pl.BlockSpec((tm, PLE_DIM), lambda t: (t, 0))]
                 + [_resident(a.shape) for a in params],
        out_specs=tok_spec,
        out_shape=jax.ShapeDtypeStruct((tokens, D_MODEL), F32),
        compiler_params=pltpu.CompilerParams(
            dimension_semantics=("arbitrary",), vmem_limit_bytes=V7X_VMEM_LIMIT_BYTES),
        name="merge_ffn",
    )(x2, ya, yb, p2, *params)


def _rope_tables(seq):
    pos = jnp.arange(seq, dtype=F32)
    inv = 1.0 / (ROPE_THETA ** (jnp.arange(0, HEAD_DIM, 2, dtype=F32) / HEAD_DIM))
    ang = pos[:, None] * inv[None, :]
    cos = jnp.cos(ang)
    sin = jnp.sin(ang)
    cos_full = jnp.concatenate([cos, cos, cos, cos], axis=-1)
    sin_signed = jnp.concatenate([-sin, sin, -sin, sin], axis=-1)
    return cos_full, sin_signed


def kernel(x, p, norm_pre_mix, w_in, ln_v_gain, ln_v_bias, w_spatial, b_spatial, lambda_q1, lambda_k1, lambda_q2, lambda_k2, subln_gain, w_branch_a, w_branch_b, w_out, norm_post_mix, norm_pre_ffn, w_ff1, w_ff2, norm_post_ffn, w_ple_proj, w_ple_gate, b_ple_gate, norm_post_ple):
    batch, seq, d_model = x.shape
    depth = w_in.shape[0]
    assert d_model == D_MODEL and depth == 1
    assert seq % ATTN_TILE == 0 and ATTN_TILE % TOKEN_TILE == 0 and TOKEN_TILE % GMLP_BLOCK == 0
    tokens = batch * seq
    nt = seq // TOKEN_TILE
    x2 = x.reshape(tokens, D_MODEL)
    cos, sin_signed = _rope_tables(seq)

    def row(a):
        return a[0].reshape(1, -1).astype(F32)

    w_in_bf = w_in[0].astype(BF16)
    b_sp_full = jnp.repeat(b_spatial[0].T.astype(F32), D_MODEL // GMLP_GROUPS, axis=1)

    ya, qt, k, vt = _mix_in(
        x2, row(norm_pre_mix), w_in_bf[:, :MIX_WIDTH], row(ln_v_gain), row(ln_v_bias),
        w_spatial[0].astype(F32), b_sp_full, cos, sin_signed, seq)

    lam_params = jnp.concatenate([lambda_q1, lambda_k1, lambda_q2, lambda_k2], axis=0).astype(F32)
    yb = _diff_attn(qt.reshape(batch, nt, D_MODEL, TOKEN_TILE), k.reshape(batch, seq, D_MODEL),
                    vt.reshape(batch, nt, D_MODEL, TOKEN_TILE), lam_params, row(subln_gain))

    params = (
        row(norm_pre_mix), w_in_bf[:, MIX_WIDTH:], w_branch_a[0].astype(BF16),
        w_branch_b[0].astype(BF16), w_out[0].astype(BF16), row(norm_post_mix),
        row(norm_pre_ffn), w_ff1[0].astype(BF16), w_ff2[0].astype(BF16), row(norm_post_ffn),
        w_ple_proj[0].astype(BF16), w_ple_gate[0].astype(BF16), row(b_ple_gate), row(norm_post_ple),
    )
    out = _merge_ffn(x2, ya, yb.reshape(tokens, D_MODEL), p[0].reshape(tokens, PLE_DIM), params)
    return out.reshape(batch, seq, D_MODEL)
```

```python
import math

import jax
import jax.numpy as jnp
from jax import lax
from jax.experimental import pallas as pl
from jax.experimental.pallas import tpu as pltpu

D_MODEL = 1024
CHUNK = 64
GMLP_BLOCK = 128
GMLP_GROUPS = 8
HEAD_DIM = 64
HEADS = 8
HEAD_WIDTH = 2 * HEAD_DIM
FF_WIDTH = 4 * D_MODEL
PLE_DIM = 256
ROPE_THETA = 10000.0
EPS = 1e-6
LAMBDA_INIT = 0.8 - 0.6 * math.exp(-0.3 * 0)
Q_SCALE = HEAD_DIM ** -0.5 * math.log2(math.e)

MIX_WIDTH = 5 * D_MODEL
V7X_VMEM_LIMIT_BYTES = 56 * 1024 * 1024
BF16_SUBLANES = 16

TOKEN_TILE = 512
MERGE_TILE = 256
ATTN_TILE = 512
ATTN_HEADS = 4

F32 = jnp.float32
BF16 = jnp.bfloat16


def _rms(x, gain):
    return x * lax.rsqrt(jnp.mean(x * x, axis=-1, keepdims=True) + EPS) * gain


def _dot(a, b):
    return jnp.dot(a, b, preferred_element_type=F32)


def _resident(shape):
    zeros = (0,) * len(shape)
    return pl.BlockSpec(shape, lambda *_: zeros, pipeline_mode=pl.Buffered(1))


def _mix_in_kernel(x_ref, g_ref, w_ref, lng_ref, lnb_ref, wsp_ref, bsp_ref, cos_ref, sin_ref,
                   ya_ref, qt_ref, k_ref, vt_ref):
    tm = x_ref.shape[0]
    n = _rms(x_ref[...], g_ref[...]).astype(BF16)

    gu = jax.nn.gelu(_dot(n, w_ref[:, 0:D_MODEL]))
    gv = jax.nn.gelu(_dot(n, w_ref[:, D_MODEL:2 * D_MODEL]))
    mu = jnp.mean(gv, axis=-1, keepdims=True)
    cen = gv - mu
    var = jnp.mean(cen * cen, axis=-1, keepdims=True)
    vn = (cen * lax.rsqrt(var + EPS) * lng_ref[...] + lnb_ref[...]).astype(BF16)

    va = _dot(n, w_ref[:, 4 * D_MODEL:5 * D_MODEL])
    for h in range(HEADS):
        hs = slice(h * HEAD_WIDTH, (h + 1) * HEAD_WIDTH)
        vt_ref[hs, :] = va[:, hs].T.astype(BF16)

    cos = cos_ref[...]
    sin = sin_ref[...]
    lane = lax.broadcasted_iota(jnp.int32, (tm, HEAD_WIDTH), 1)
    low_half = (lane % HEAD_DIM) < (HEAD_DIM // 2)

    def rope(t, h):
        th = t[:, h * HEAD_WIDTH:(h + 1) * HEAD_WIDTH]
        rot = jnp.where(low_half,
                        pltpu.roll(th, HEAD_WIDTH - HEAD_DIM // 2, 1),
                        pltpu.roll(th, HEAD_DIM // 2, 1))
        return th * cos + rot * sin

    t = _dot(n, w_ref[:, 2 * D_MODEL:3 * D_MODEL])
    for h in range(HEADS):
        hs = slice(h * HEAD_WIDTH, (h + 1) * HEAD_WIDTH)
        qt_ref[hs, :] = (rope(t, h) * Q_SCALE).T.astype(BF16)
    t = _dot(n, w_ref[:, 3 * D_MODEL:4 * D_MODEL])
    for h in range(HEADS):
        hs = slice(h * HEAD_WIDTH, (h + 1) * HEAD_WIDTH)
        k_ref[:, hs] = rope(t, h).astype(BF16)

    row = lax.broadcasted_iota(jnp.int32, (GMLP_BLOCK, GMLP_BLOCK), 0)
    col = lax.broadcasted_iota(jnp.int32, (GMLP_BLOCK, GMLP_BLOCK), 1)
    allowed = (col // CHUNK) <= (row // CHUNK)
    gw = D_MODEL // GMLP_GROUPS
    for g in range(GMLP_GROUPS):
        w_g = jnp.where(allowed, wsp_ref[g], 0.0).astype(BF16)
        cs = slice(g * gw, (g + 1) * gw)
        for blk in range(tm // GMLP_BLOCK):
            rs = slice(blk * GMLP_BLOCK, (blk + 1) * GMLP_BLOCK)
            mixed = _dot(w_g, vn[rs, cs]) + bsp_ref[:, cs]
            ya_ref[rs, cs] = (gu[rs, cs] * mixed).astype(BF16)


def _mix_in(x2, g_pre, w_mix, ln_g, ln_b, w_sp, b_sp_full, cos, sin_signed, seq):
    tokens = x2.shape[0]
    tm = TOKEN_TILE
    tiles = tokens // tm
    tiles_per_seq = seq // tm
    tok_spec = pl.BlockSpec((tm, D_MODEL), lambda t: (t, 0))
    pos_spec = pl.BlockSpec((tm, HEAD_WIDTH), lambda t: (t % tiles_per_seq, 0))
    feat_spec = pl.BlockSpec((None, D_MODEL, tm), lambda t: (t, 0, 0))
    tok_out = jax.ShapeDtypeStruct((tokens, D_MODEL), BF16)
    feat_out = jax.ShapeDtypeStruct((tiles, D_MODEL, tm), BF16)
    return pl.pallas_call(
        _mix_in_kernel,
        grid=(tiles,),
        in_specs=[
            tok_spec,
            _resident((1, D_MODEL)),
            _resident((D_MODEL, MIX_WIDTH)),
            _resident((1, D_MODEL)),
            _resident((1, D_MODEL)),
            _resident((GMLP_GROUPS, GMLP_BLOCK, GMLP_BLOCK)),
            _resident((GMLP_BLOCK, D_MODEL)),
            pos_spec,
            pos_spec,
        ],
        out_specs=[tok_spec, feat_spec, tok_spec, feat_spec],
        out_shape=[tok_out, feat_out, tok_out, feat_out],
        compiler_params=pltpu.CompilerParams(
            dimension_semantics=("arbitrary",), vmem_limit_bytes=V7X_VMEM_LIMIT_BYTES),
        name="mix_in",
    )(x2, g_pre, w_mix, ln_g, ln_b, w_sp, b_sp_full, cos, sin_signed)


def _diff_attn_kernel(qt_ref, k_ref, vt_ref, lam_ref, g_ref, o_ref, *head_scratch):
    t = o_ref.shape[0]
    slabs = qt_ref.shape[0]
    heads = qt_ref.shape[1] // HEAD_WIDTH
    w = 2 * t
    s_refs = head_scratch[0::3]
    p_refs = head_scratch[1::3]
    acc_refs = head_scratch[2::3]
    i = pl.program_id(2)
    every_head = range(heads)

    def head_rows(g):
        return slice(g * HEAD_WIDTH, (g + 1) * HEAD_WIDTH)

    def head_lanes(g):
        return slice(g * w, (g + 1) * w)

    feat = lax.broadcasted_iota(jnp.int32, (HEAD_WIDTH, t), 0)

    def query_operand(g):
        qt = jnp.concatenate([qt_ref[r, head_rows(g), :] for r in range(slabs)], axis=1)
        zero = jnp.zeros_like(qt)
        return jnp.concatenate([jnp.where(feat < HEAD_DIM, qt, zero),
                                jnp.where(feat >= HEAD_DIM, qt, zero)], axis=1)

    ones = jnp.ones((BF16_SUBLANES, t), BF16)

    def qk(g, qt2, j):
        s = _dot(k_ref[pl.ds(pl.multiple_of(j * t, t), t), head_rows(g)], qt2)
        s_refs[g][...] = s
        return jnp.max(s, axis=0, keepdims=True)

    def softmax_block(g, s_max, m):
        m_new = jnp.maximum(m, s_max)
        p_refs[g][...] = jnp.exp2(s_refs[g][...] - m_new).astype(BF16)
        return m_new, jnp.exp2(m - m_new)

    def value_operand(g, j):
        vt = jnp.concatenate(
            [vt_ref[slabs * j + r, head_rows(g), :] for r in range(slabs)], axis=1)
        return jnp.concatenate([vt, ones], axis=0)

    def pv(g, j, alpha):
        acc_refs[g][...] = alpha * acc_refs[g][...] + _dot(value_operand(g, j), p_refs[g][...])

    qt2 = [query_operand(g) for g in every_head]
    qry = lax.broadcasted_iota(jnp.int32, (1, w), 1) % t
    neg_inf = jnp.full((1, w), -jnp.inf, F32)

    def finish(g, m, first):
        def band(c):
            return jnp.where(qry >= c * CHUNK, s_refs[g][c * CHUNK:(c + 1) * CHUNK, :], -jnp.inf)

        m_new = m
        for c in range(t // CHUNK):
            m_new = jnp.maximum(m_new, jnp.max(band(c), axis=0, keepdims=True))
        for c in range(t // CHUNK):
            p_refs[g][c * CHUNK:(c + 1) * CHUNK, :] = jnp.exp2(band(c) - m_new).astype(BF16)
        if first:
            acc_refs[g][...] = _dot(value_operand(g, 0), p_refs[g][...])
        else:
            pv(g, i, jnp.exp2(m - m_new))

        lam_p = lam_ref[...]
        lam = (jnp.exp(jnp.sum(lam_p[0:1] * lam_p[1:2], axis=-1, keepdims=True))
               - jnp.exp(jnp.sum(lam_p[2:3] * lam_p[3:4], axis=-1, keepdims=True))
               + LAMBDA_INIT)
        inv_l = 1.0 / acc_refs[g][HEAD_WIDTH:HEAD_WIDTH + 1, :]
        o = (acc_refs[g][:HEAD_WIDTH, :t] * inv_l[:, :t]
             - lam * (acc_refs[g][:HEAD_WIDTH, t:] * inv_l[:, t:]))
        y = o * lax.rsqrt(jnp.mean(o * o, axis=0, keepdims=True) + EPS)
        o_ref[:, head_rows(g)] = (y.T * g_ref[...] * (1.0 - LAMBDA_INIT)).astype(BF16)

    @pl.when(i == 0)
    def _():
        for g in every_head:
            qk(g, qt2[g], 0)
        for g in every_head:
            finish(g, neg_inf, first=True)

    @pl.when(i > 0)
    def _():
        s_max = [qk(g, qt2[g], 0) for g in every_head]
        for g in every_head:
            acc_refs[g][...] = jnp.zeros_like(acc_refs[g])
        stats = [softmax_block(g, s_max[g], neg_inf) for g in every_head]
        s_max = [qk(g, qt2[g], 1) for g in every_head]

        def kv_step(j, carry):
            s_max, alpha, m = carry
            for g in every_head:
                pv(g, j - 1, alpha[g])
            stats = [softmax_block(g, s_max[g], m[g]) for g in every_head]
            s_max = [qk(g, qt2[g], j + 1) for g in every_head]
            return s_max, [a for _, a in stats], [mm for mm, _ in stats]

        init = (s_max, [a for _, a in stats], [mm for mm, _ in stats])
        _, alpha, m = lax.fori_loop(1, i, kv_step, init)
        for g in every_head:
            pv(g, i - 1, alpha[g])
        for g in every_head:
            finish(g, m[g], first=False)


def _diff_attn(qt, k, vt, lam_params, subln_gain):
    batch, seq, _ = k.shape
    t = ATTN_TILE
    gw = ATTN_HEADS * HEAD_WIDTH
    w = 2 * t
    slabs = t // TOKEN_TILE
    head_scratch = [
        pltpu.VMEM((t, w), F32),
        pltpu.VMEM((t, w), BF16),
        pltpu.VMEM((HEAD_WIDTH + BF16_SUBLANES, w), F32),
    ]
    return pl.pallas_call(
        _diff_attn_kernel,
        grid=(batch, HEADS // ATTN_HEADS, seq // t),
        in_specs=[
            pl.BlockSpec((None, slabs, gw, TOKEN_TILE), lambda b, h, i: (b, i, h, 0)),
            pl.BlockSpec((None, seq, gw), lambda b, h, i: (b, 0, h)),
            pl.BlockSpec((None, seq // TOKEN_TILE, gw, TOKEN_TILE), lambda b, h, i: (b, 0, h, 0)),
            pl.BlockSpec((4, HEAD_DIM), lambda b, h, i: (0, 0)),
            pl.BlockSpec((1, HEAD_WIDTH), lambda b, h, i: (0, 0)),
        ],
        out_specs=pl.BlockSpec((None, t, gw), lambda b, h, i: (b, i, h)),
        out_shape=jax.ShapeDtypeStruct((batch, seq, D_MODEL), BF16),
        scratch_shapes=head_scratch * ATTN_HEADS,
        compiler_params=pltpu.CompilerParams(
            dimension_semantics=("arbitrary", "arbitrary", "arbitrary"),
            vmem_limit_bytes=V7X_VMEM_LIMIT_BYTES),
        name="diff_attn",
    )(qt, k, vt, lam_params, subln_gain)


def _merge_ffn_kernel(x_ref, ya_ref, yb_ref, p_ref,
                      g_pre_ref, w_gate_ref, w_a_ref, w_b_ref, w_out_ref, g_post_mix_ref,
                      g_pre_ffn_ref, w_ff1_ref, w_ff2_ref, g_post_ffn_ref,
                      w_ple_proj_ref, w_ple_gate_ref, b_ple_gate_ref, g_post_ple_ref,
                      o_ref):
    x = x_ref[...]
    n = _rms(x, g_pre_ref[...]).astype(BF16)
    gates = jax.nn.sigmoid(_dot(n, w_gate_ref[...]))
    merged = (gates[:, :D_MODEL] * _dot(ya_ref[...], w_a_ref[...])
              + gates[:, D_MODEL:] * _dot(yb_ref[...], w_b_ref[...]))
    h = x + _rms(_dot(merged.astype(BF16), w_out_ref[...]), g_post_mix_ref[...])

    f = _dot(_rms(h, g_pre_ffn_ref[...]).astype(BF16), w_ff1_ref[...])
    f = jnp.square(jnp.maximum(f, 0.0))
    h = h + _rms(_dot(f.astype(BF16), w_ff2_ref[...]), g_post_ffn_ref[...])

    e = (_dot(p_ref[...].astype(BF16), w_ple_proj_ref[...])
         * jax.nn.sigmoid(_dot(h.astype(BF16), w_ple_gate_ref[...]) + b_ple_gate_ref[...]))
    o_ref[...] = h + _rms(e, g_post_ple_ref[...])


def _merge_ffn(x2, ya, yb, p2, params):
    tokens = x2.shape[0]
    tm = MERGE_TILE
    tok_spec = pl.BlockSpec((tm, D_MODEL), lambda t: (t, 0))
    return pl.pallas_call(
        _merge_ffn_kernel,
        grid=(tokens // tm,),
        in_specs=[tok_spec, tok_spec, tok_spec, pl.BlockSpec((tm, PLE_DIM), lambda t: (t, 0))]
                 + [_resident(a.shape) for a in params],
        out_specs=tok_spec,
        out_shape=jax.ShapeDtypeStruct((tokens, D_MODEL), F32),
        compiler_params=pltpu.CompilerParams(
            dimension_semantics=("arbitrary",), vmem_limit_bytes=V7X_VMEM_LIMIT_BYTES),
        name="merge_ffn",
    )(x2, ya, yb, p2, *params)


def _rope_tables(seq):
    pos = jnp.arange(seq, dtype=F32)
    inv = 1.0 / (ROPE_THETA ** (jnp.arange(0, HEAD_DIM, 2, dtype=F32) / HEAD_DIM))
    ang = pos[:, None] * inv[None, :]
    cos = jnp.cos(ang)
    sin = jnp.sin(ang)
    cos_full = jnp.concatenate([cos, cos, cos, cos], axis=-1)
    sin_signed = jnp.concatenate([-sin, sin, -sin, sin], axis=-1)
    return cos_full, sin_signed


def kernel(x, p, norm_pre_mix, w_in, ln_v_gain, ln_v_bias, w_spatial, b_spatial, lambda_q1, lambda_k1, lambda_q2, lambda_k2, subln_gain, w_branch_a, w_branch_b, w_out, norm_post_mix, norm_pre_ffn, w_ff1, w_ff2, norm_post_ffn, w_ple_proj, w_ple_gate, b_ple_gate, norm_post_ple):
    batch, seq, d_model = x.shape
    depth = w_in.shape[0]
    assert d_model == D_MODEL and depth == 1
    assert seq % ATTN_TILE == 0 and ATTN_TILE % TOKEN_TILE == 0 and TOKEN_TILE % GMLP_BLOCK == 0
    tokens = batch * seq
    nt = seq // TOKEN_TILE
    x2 = x.reshape(tokens, D_MODEL)
    cos, sin_signed = _rope_tables(seq)

    def row(a):
        return a[0].reshape(1, -1).astype(F32)

    w_in_bf = w_in[0].astype(BF16)
    b_sp_full = jnp.repeat(b_spatial[0].T.astype(F32), D_MODEL // GMLP_GROUPS, axis=1)

    ya, qt, k, vt = _mix_in(
        x2, row(norm_pre_mix), w_in_bf[:, :MIX_WIDTH], row(ln_v_gain), row(ln_v_bias),
        w_spatial[0].astype(F32), b_sp_full, cos, sin_signed, seq)

    lam_params = jnp.concatenate([lambda_q1, lambda_k1, lambda_q2, lambda_k2], axis=0).astype(F32)
    yb = _diff_attn(qt.reshape(batch, nt, D_MODEL, TOKEN_TILE), k.reshape(batch, seq, D_MODEL),
                    vt.reshape(batch, nt, D_MODEL, TOKEN_TILE), lam_params, row(subln_gain))

    params = (
        row(norm_pre_mix), w_in_bf[:, MIX_WIDTH:], w_branch_a[0].astype(BF16),
        w_branch_b[0].astype(BF16), w_out[0].astype(BF16), row(norm_post_mix),
        row(norm_pre_ffn), w_ff1[0].astype(BF16), w_ff2[0].astype(BF16), row(norm_post_ffn),
        w_ple_proj[0].astype(BF16), w_ple_gate[0].astype(BF16), row(b_ple_gate), row(norm_post_ple),
    )
    out = _merge_ffn(x2, ya, yb.reshape(tokens, D_MODEL), p[0].reshape(tokens, PLE_DIM), params)
    return out.reshape(batch, seq, D_MODEL)
```

```python
import math

import jax
import jax.numpy as jnp
from jax import lax
from jax.experimental import pallas as pl
from jax.experimental.pallas import tpu as pltpu

D_MODEL = 1024
CHUNK = 64
GMLP_BLOCK = 128
GMLP_GROUPS = 8
HEAD_DIM = 64
HEADS = 8
HEAD_WIDTH = 2 * HEAD_DIM
FF_WIDTH = 4 * D_MODEL
PLE_DIM = 256
ROPE_THETA = 10000.0
EPS = 1e-6
LAMBDA_INIT = 0.8 - 0.6 * math.exp(-0.3 * 0)
Q_SCALE = HEAD_DIM ** -0.5 * math.log2(math.e)

MIX_WIDTH = 5 * D_MODEL
V7X_VMEM_LIMIT_BYTES = 56 * 1024 * 1024
BF16_SUBLANES = 16

TOKEN_TILE = 512
MERGE_TILE = 512
MERGE_ROWS = 256
ATTN_TILE = 512
ATTN_HEADS = 4

F32 = jnp.float32
BF16 = jnp.bfloat16


def _rms(x, gain):
    return x * lax.rsqrt(jnp.mean(x * x, axis=-1, keepdims=True) + EPS) * gain


def _dot(a, b):
    return jnp.dot(a, b, preferred_element_type=F32)


def _resident(shape):
    zeros = (0,) * len(shape)
    return pl.BlockSpec(shape, lambda *_: zeros, pipeline_mode=pl.Buffered(1))


def _mix_in_kernel(x_ref, g_ref, w_ref, lng_ref, lnb_ref, wsp_ref, bsp_ref, cos_ref, sin_ref,
                   ya_ref, qt_ref, k_ref, vt_ref):
    tm = x_ref.shape[0]
    n = _rms(x_ref[...], g_ref[...]).astype(BF16)

    gu = jax.nn.gelu(_dot(n, w_ref[:, 0:D_MODEL]))
    gv = jax.nn.gelu(_dot(n, w_ref[:, D_MODEL:2 * D_MODEL]))
    mu = jnp.mean(gv, axis=-1, keepdims=True)
    cen = gv - mu
    var = jnp.mean(cen * cen, axis=-1, keepdims=True)
    vn = (cen * lax.rsqrt(var + EPS) * lng_ref[...] + lnb_ref[...]).astype(BF16)

    va = _dot(n, w_ref[:, 4 * D_MODEL:5 * D_MODEL])
    for h in range(HEADS):
        hs = slice(h * HEAD_WIDTH, (h + 1) * HEAD_WIDTH)
        vt_ref[hs, :] = va[:, hs].T.astype(BF16)

    cos = cos_ref[...]
    sin = sin_ref[...]

    def rope(t, h):
        th = t[:, h * HEAD_WIDTH:(h + 1) * HEAD_WIDTH]
        return th * cos + pltpu.roll(th, HEAD_WIDTH // 2, 1) * sin

    t = _dot(n, w_ref[:, 2 * D_MODEL:3 * D_MODEL])
    for h in range(HEADS):
        hs = slice(h * HEAD_WIDTH, (h + 1) * HEAD_WIDTH)
        qt_ref[hs, :] = (rope(t, h) * Q_SCALE).T.astype(BF16)
    row = lax.broadcasted_iota(jnp.int32, (GMLP_BLOCK, GMLP_BLOCK), 0)
    col = lax.broadcasted_iota(jnp.int32, (GMLP_BLOCK, GMLP_BLOCK), 1)
    allowed = (col // CHUNK) <= (row // CHUNK)
    gw = D_MODEL // GMLP_GROUPS
    blocks = [slice(r, r + GMLP_BLOCK) for r in range(0, tm, GMLP_BLOCK)]
    for g in range(GMLP_GROUPS):
        w_g = jnp.where(allowed, wsp_ref[g], 0.0).astype(BF16)
        cs = slice(g * gw, (g + 1) * gw)
        mixed = _dot(w_g, jnp.concatenate([vn[rs, cs] for rs in blocks], axis=1))
        for b, rs in enumerate(blocks):
            mixed_b = mixed[:, b * gw:(b + 1) * gw] + bsp_ref[:, cs]
            ya_ref[rs, cs] = (gu[rs, cs] * mixed_b).astype(BF16)

    t = _dot(n, w_ref[:, 3 * D_MODEL:4 * D_MODEL])
    for h in range(HEADS):
        hs = slice(h * HEAD_WIDTH, (h + 1) * HEAD_WIDTH)
        k_ref[:, hs] = rope(t, h).astype(BF16)


def _mix_in(x2, g_pre, w_mix, ln_g, ln_b, w_sp, b_sp_full, cos, sin_signed, seq):
    tokens = x2.shape[0]
    tm = TOKEN_TILE
    tiles = tokens // tm
    tiles_per_seq = seq // tm
    tok_spec = pl.BlockSpec((tm, D_MODEL), lambda t: (t, 0))
    pos_spec = pl.BlockSpec((tm, HEAD_WIDTH), lambda t: (t % tiles_per_seq, 0))
    feat_spec = pl.BlockSpec((None, D_MODEL, tm), lambda t: (t, 0, 0))
    tok_out = jax.ShapeDtypeStruct((tokens, D_MODEL), BF16)
    feat_out = jax.ShapeDtypeStruct((tiles, D_MODEL, tm), BF16)
    return pl.pallas_call(
        _mix_in_kernel,
        grid=(tiles,),
        in_specs=[
            tok_spec,
            _resident((1, D_MODEL)),
            _resident((D_MODEL, MIX_WIDTH)),
            _resident((1, D_MODEL)),
            _resident((1, D_MODEL)),
            _resident((GMLP_GROUPS, GMLP_BLOCK, GMLP_BLOCK)),
            _resident((GMLP_BLOCK, D_MODEL)),
            pos_spec,
            pos_spec,
        ],
        out_specs=[tok_spec, feat_spec, tok_spec, feat_spec],
        out_shape=[tok_out, feat_out, tok_out, feat_out],
        compiler_params=pltpu.CompilerParams(
            dimension_semantics=("arbitrary",), vmem_limit_bytes=V7X_VMEM_LIMIT_BYTES),
        name="mix_in",
    )(x2, g_pre, w_mix, ln_g, ln_b, w_sp, b_sp_full, cos, sin_signed)


def _diff_attn_kernel(qt_ref, k_ref, vt_ref, lam_ref, g_ref, o_ref, *head_scratch):
    t = o_ref.shape[0]
    slabs = qt_ref.shape[0]
    heads = qt_ref.shape[1] // HEAD_WIDTH
    w = 2 * t
    s_refs = head_scratch[0::3]
    p_refs = head_scratch[1::3]
    acc_refs = head_scratch[2::3]
    i = pl.program_id(2)
    every_head = range(heads)

    def head_rows(g):
        return slice(g * HEAD_WIDTH, (g + 1) * HEAD_WIDTH)

    def head_lanes(g):
        return slice(g * w, (g + 1) * w)

    feat = lax.broadcasted_iota(jnp.int32, (HEAD_WIDTH, t), 0)

    def query_operand(g):
        qt = jnp.concatenate([qt_ref[r, head_rows(g), :] for r in range(slabs)], axis=1)
        zero = jnp.zeros_like(qt)
        first = (feat // (HEAD_DIM // 2)) % 2 == 0
        return jnp.concatenate([jnp.where(first, qt, zero), jnp.where(first, zero, qt)], axis=1)

    ones = jnp.ones((BF16_SUBLANES, t), BF16)

    def qk(g, qt2, j):
        s = _dot(k_ref[pl.ds(pl.multiple_of(j * t, t), t), head_rows(g)], qt2)
        s_refs[g][...] = s
        return jnp.max(s, axis=0, keepdims=True)

    def softmax_block(g, s_max, m):
        m_new = jnp.maximum(m, s_max)
        p_refs[g][...] = jnp.exp2(s_refs[g][...] - m_new).astype(BF16)
        return m_new, jnp.exp2(m - m_new)

    def value_operand(g, j):
        vt = jnp.concatenate(
            [vt_ref[slabs * j + r, head_rows(g), :] for r in range(slabs)], axis=1)
        return jnp.concatenate([vt, ones], axis=0)

    def pv(g, j, alpha):
        acc_refs[g][...] = alpha * acc_refs[g][...] + _dot(value_operand(g, j), p_refs[g][...])

    qt2 = [query_operand(g) for g in every_head]
    qry = lax.broadcasted_iota(jnp.int32, (1, w), 1) % t
    neg_inf = jnp.full((1, w), -jnp.inf, F32)

    def finish(g, m, first):
        def band(c):
            return jnp.where(qry >= c * CHUNK, s_refs[g][c * CHUNK:(c + 1) * CHUNK, :], -jnp.inf)

        m_new = m
        for c in range(t // CHUNK):
            m_new = jnp.maximum(m_new, jnp.max(band(c), axis=0, keepdims=True))
        for c in range(t // CHUNK):
            p_refs[g][c * CHUNK:(c + 1) * CHUNK, :] = jnp.exp2(band(c) - m_new).astype(BF16)
        if first:
            acc_refs[g][...] = _dot(value_operand(g, 0), p_refs[g][...])
        else:
            pv(g, i, jnp.exp2(m - m_new))

        lam_p = lam_ref[...]
        lam = (jnp.exp(jnp.sum(lam_p[0:1] * lam_p[1:2], axis=-1, keepdims=True))
               - jnp.exp(jnp.sum(lam_p[2:3] * lam_p[3:4], axis=-1, keepdims=True))
               + LAMBDA_INIT)
        inv_l = 1.0 / acc_refs[g][HEAD_WIDTH:HEAD_WIDTH + 1, :]
        o = (acc_refs[g][:HEAD_WIDTH, :t] * inv_l[:, :t]
             - lam * (acc_refs[g][:HEAD_WIDTH, t:] * inv_l[:, t:]))
        y = o * lax.rsqrt(jnp.mean(o * o, axis=0, keepdims=True) + EPS)
        o_ref[:, head_rows(g)] = (y.T * g_ref[...] * (1.0 - LAMBDA_INIT)).astype(BF16)

    @pl.when(i == 0)
    def _():
        for g in every_head:
            qk(g, qt2[g], 0)
        for g in every_head:
            finish(g, neg_inf, first=True)

    @pl.when(i > 0)
    def _():
        s_max = [qk(g, qt2[g], 0) for g in every_head]
        for g in every_head:
            acc_refs[g][...] = jnp.zeros_like(acc_refs[g])
        stats = [softmax_block(g, s_max[g], neg_inf) for g in every_head]
        s_max = [qk(g, qt2[g], 1) for g in every_head]

        def kv_step(j, carry):
            s_max, alpha, m = carry
            for g in every_head:
                pv(g, j - 1, alpha[g])
            stats = [softmax_block(g, s_max[g], m[g]) for g in every_head]
            s_max = [qk(g, qt2[g], j + 1) for g in every_head]
            return s_max, [a for _, a in stats], [mm for mm, _ in stats]

        init = (s_max, [a for _, a in stats], [mm for mm, _ in stats])
        _, alpha, m = lax.fori_loop(1, i, kv_step, init)
        for g in every_head:
            pv(g, i - 1, alpha[g])
        for g in every_head:
            finish(g, m[g], first=False)


def _diff_attn(qt, k, vt, lam_params, subln_gain):
    batch, seq, _ = k.shape
    t = ATTN_TILE
    gw = ATTN_HEADS * HEAD_WIDTH
    w = 2 * t
    slabs = t // TOKEN_TILE
    head_scratch = [
        pltpu.VMEM((t, w), F32),
        pltpu.VMEM((t, w), BF16),
        pltpu.VMEM((HEAD_WIDTH + BF16_SUBLANES, w), F32),
    ]
    return pl.pallas_call(
        _diff_attn_kernel,
        grid=(batch, HEADS // ATTN_HEADS, seq // t),
        in_specs=[
            pl.BlockSpec((None, slabs, gw, TOKEN_TILE), lambda b, h, i: (b, i, h, 0)),
            pl.BlockSpec((None, seq, gw), lambda b, h, i: (b, 0, h)),
            pl.BlockSpec((None, seq // TOKEN_TILE, gw, TOKEN_TILE), lambda b, h, i: (b, 0, h, 0)),
            pl.BlockSpec((4, HEAD_DIM), lambda b, h, i: (0, 0)),
            pl.BlockSpec((1, HEAD_WIDTH), lambda b, h, i: (0, 0)),
        ],
        out_specs=pl.BlockSpec((None, t, gw), lambda b, h, i: (b, i, h)),
        out_shape=jax.ShapeDtypeStruct((batch, seq, D_MODEL), BF16),
        scratch_shapes=head_scratch * ATTN_HEADS,
        compiler_params=pltpu.CompilerParams(
            dimension_semantics=("arbitrary", "arbitrary", "arbitrary"),
            vmem_limit_bytes=V7X_VMEM_LIMIT_BYTES),
        name="diff_attn",
    )(qt, k, vt, lam_params, subln_gain)


def _merge_ffn_kernel(x_ref, ya_ref, yb_ref, p_ref,
                      g_pre_ref, w_gate_ref, w_a_ref, w_b_ref, w_out_ref, g_post_mix_ref,
                      g_pre_ffn_ref, w_ff1_ref, w_ff2_ref, g_post_ffn_ref,
                      w_ple_proj_ref, w_ple_gate_ref, b_ple_gate_ref, g_post_ple_ref,
                      o_ref):
    tm = x_ref.shape[0]
    groups = [slice(r, r + MERGE_ROWS) for r in range(0, tm, MERGE_ROWS)]

    def out_proj(rows):
        n = _rms(x_ref[rows, :], g_pre_ref[...]).astype(BF16)
        gates = jax.nn.sigmoid(_dot(n, w_gate_ref[...]))
        merged = (gates[:, :D_MODEL] * _dot(ya_ref[rows, :], w_a_ref[...])
                  + gates[:, D_MODEL:] * _dot(yb_ref[rows, :], w_b_ref[...]))
        return _dot(merged.astype(BF16), w_out_ref[...])

    def ff_in(rows, mixed):
        h = x_ref[rows, :] + _rms(mixed, g_post_mix_ref[...])
        f = _dot(_rms(h, g_pre_ffn_ref[...]).astype(BF16), w_ff1_ref[...])
        return h, jnp.square(jnp.maximum(f, 0.0)).astype(BF16)

    def ff_out(f):
        return _dot(f, w_ff2_ref[...])

    def embed(rows, h, f):
        h = h + _rms(f, g_post_ffn_ref[...])
        e = (_dot(p_ref[rows, :].astype(BF16), w_ple_proj_ref[...])
             * jax.nn.sigmoid(_dot(h.astype(BF16), w_ple_gate_ref[...]) + b_ple_gate_ref[...]))
        o_ref[rows, :] = h + _rms(e, g_post_ple_ref[...])

    mixed = [out_proj(rows) for rows in groups]
    hf = [ff_in(rows, m) for rows, m in zip(groups, mixed)]
    f2 = [ff_out(f) for _, f in hf]
    for rows, (h, _), f in zip(groups, hf, f2):
        embed(rows, h, f)


def _merge_ffn(x2, ya, yb, p2, params):
    tokens = x2.shape[0]
    tm = MERGE_TILE
    tok_spec = pl.BlockSpec((tm, D_MODEL), lambda t: (t, 0))
    return pl.pallas_call(
        _merge_ffn_kernel,
        grid=(tokens // tm,),
        in_specs=[tok_spec, tok_spec, tok_spec, pl.BlockSpec((tm, PLE_DIM), lambda t: (t, 0))]
                 + [_resident(a.shape) for a in params],
        out_specs=tok_spec,
        out_shape=jax.ShapeDtypeStruct((tokens, D_MODEL), F32),
        compiler_params=pltpu.CompilerParams(
            dimension_semantics=("arbitrary",), vmem_limit_bytes=V7X_VMEM_LIMIT_BYTES),
        name="merge_ffn",
    )(x2, ya, yb, p2, *params)


def _rope_tables(seq):
    pos = jnp.arange(seq, dtype=F32)
    inv = 1.0 / (ROPE_THETA ** (jnp.arange(0, HEAD_DIM, 2, dtype=F32) / HEAD_DIM))
    ang = pos[:, None] * inv[None, :]
    cos = jnp.cos(ang)
    sin = jnp.sin(ang)
    cos_full = jnp.concatenate([cos, cos, cos, cos], axis=-1)
    sin_signed = jnp.concatenate([-sin, -sin, sin, sin], axis=-1)
    return cos_full, sin_signed


def _rotary_layout(w):
    rows = w.shape[0]
    w = w.reshape(rows, HEADS, 2, 2, HEAD_DIM // 2)
    return w.transpose(0, 1, 3, 2, 4).reshape(rows, HEADS * HEAD_WIDTH)


def kernel(x, p, norm_pre_mix, w_in, ln_v_gain, ln_v_bias, w_spatial, b_spatial, lambda_q1, lambda_k1, lambda_q2, lambda_k2, subln_gain, w_branch_a, w_branch_b, w_out, norm_post_mix, norm_pre_ffn, w_ff1, w_ff2, norm_post_ffn, w_ple_proj, w_ple_gate, b_ple_gate, norm_post_ple):
    batch, seq, d_model = x.shape
    depth = w_in.shape[0]
    assert d_model == D_MODEL and depth == 1
    assert seq % ATTN_TILE == 0 and ATTN_TILE % TOKEN_TILE == 0 and TOKEN_TILE % GMLP_BLOCK == 0
    tokens = batch * seq
    nt = seq // TOKEN_TILE
    x2 = x.reshape(tokens, D_MODEL)
    cos, sin_signed = _rope_tables(seq)

    def row(a):
        return a[0].reshape(1, -1).astype(F32)

    w_in_bf = w_in[0].astype(BF16)
    w_mix = jnp.concatenate([
        w_in_bf[:, :2 * D_MODEL],
        _rotary_layout(w_in_bf[:, 2 * D_MODEL:3 * D_MODEL]),
        _rotary_layout(w_in_bf[:, 3 * D_MODEL:4 * D_MODEL]),
        w_in_bf[:, 4 * D_MODEL:MIX_WIDTH]], axis=1)
    b_sp_full = jnp.repeat(b_spatial[0].T.astype(F32), D_MODEL // GMLP_GROUPS, axis=1)

    ya, qt, k, vt = _mix_in(
        x2, row(norm_pre_mix), w_mix, row(ln_v_gain), row(ln_v_bias),
        w_spatial[0].astype(F32), b_sp_full, cos, sin_signed, seq)

    lam_params = jnp.concatenate([lambda_q1, lambda_k1, lambda_q2, lambda_k2], axis=0).astype(F32)
    yb = _diff_attn(qt.reshape(batch, nt, D_MODEL, TOKEN_TILE), k.reshape(batch, seq, D_MODEL),
                    vt.reshape(batch, nt, D_MODEL, TOKEN_TILE), lam_params, row(subln_gain))

    params = (
        row(norm_pre_mix), w_in_bf[:, MIX_WIDTH:], w_branch_a[0].astype(BF16),
        w_branch_b[0].astype(BF16), w_out[0].astype(BF16), row(norm_post_mix),
        row(norm_pre_ffn), w_ff1[0].astype(BF16), w_ff2[0].astype(BF16), row(norm_post_ffn),
        w_ple_proj[0].astype(BF16), w_ple_gate[0].astype(BF16), row(b_ple_gate), row(norm_post_ple),
    )
    out = _merge_ffn(x2, ya, yb.reshape(tokens, D_MODEL), p[0].reshape(tokens, PLE_DIM), params)
    return out.reshape(batch, seq, D_MODEL)
```

```python
import math

import jax
import jax.numpy as jnp
from jax import lax
from jax.experimental import pallas as pl
from jax.experimental.pallas import tpu as pltpu

D_MODEL = 1024
CHUNK = 64
GMLP_BLOCK = 128
GMLP_GROUPS = 8
HEAD_DIM = 64
HEADS = 8
HEAD_WIDTH = 2 * HEAD_DIM
FF_WIDTH = 4 * D_MODEL
PLE_DIM = 256
ROPE_THETA = 10000.0
EPS = 1e-6
LAMBDA_INIT = 0.8 - 0.6 * math.exp(-0.3 * 0)
Q_SCALE = HEAD_DIM ** -0.5 * math.log2(math.e)

MIX_WIDTH = 5 * D_MODEL
V7X_VMEM_LIMIT_BYTES = 56 * 1024 * 1024
BF16_SUBLANES = 16

TOKEN_TILE = 512
MIX_ROWS = 256
MERGE_TILE = 512
MERGE_ROWS = 256
ATTN_TILE = 512
ATTN_HEADS = 4

F32 = jnp.float32
BF16 = jnp.bfloat16


def _rms(x, gain):
    return x * lax.rsqrt(jnp.mean(x * x, axis=-1, keepdims=True) + EPS) * gain


def _dot(a, b):
    return jnp.dot(a, b, preferred_element_type=F32)


def _resident(shape):
    zeros = (0,) * len(shape)
    return pl.BlockSpec(shape, lambda *_: zeros, pipeline_mode=pl.Buffered(1))


def _mix_in_kernel(x_ref, g_ref, w_ref, lng_ref, lnb_ref, wsp_ref, bsp_ref, cos_ref, sin_ref,
                   ya_ref, qt_ref, k_ref, vt_ref):
    tm = x_ref.shape[0]
    groups = [slice(r, r + MIX_ROWS) for r in range(0, tm, MIX_ROWS)]
    heads = [slice(h * HEAD_WIDTH, (h + 1) * HEAD_WIDTH) for h in range(HEADS)]

    def gate_inputs(rows):
        n = _rms(x_ref[rows, :], g_ref[...]).astype(BF16)
        gu = jax.nn.gelu(_dot(n, w_ref[:, 0:D_MODEL]))
        gv = jax.nn.gelu(_dot(n, w_ref[:, D_MODEL:2 * D_MODEL]))
        mu = jnp.mean(gv, axis=-1, keepdims=True)
        cen = gv - mu
        var = jnp.mean(cen * cen, axis=-1, keepdims=True)
        vn = (cen * lax.rsqrt(var + EPS) * lng_ref[...] + lnb_ref[...]).astype(BF16)
        return n, gu, vn

    def rope(rows, t, hs):
        th = t[:, hs]
        return th * cos_ref[rows, :] + pltpu.roll(th, HEAD_WIDTH // 2, 1) * sin_ref[rows, :]

    def values_and_queries(rows, n):
        va = _dot(n, w_ref[:, 4 * D_MODEL:5 * D_MODEL])
        for hs in heads:
            vt_ref[hs, rows] = va[:, hs].T.astype(BF16)
        t = _dot(n, w_ref[:, 2 * D_MODEL:3 * D_MODEL])
        for hs in heads:
            qt_ref[hs, rows] = (rope(rows, t, hs) * Q_SCALE).T.astype(BF16)

    row = lax.broadcasted_iota(jnp.int32, (GMLP_BLOCK, GMLP_BLOCK), 0)
    col = lax.broadcasted_iota(jnp.int32, (GMLP_BLOCK, GMLP_BLOCK), 1)
    allowed = (col // CHUNK) <= (row // CHUNK)
    gw = D_MODEL // GMLP_GROUPS

    def spatial_gate_and_keys(rows, n, gu, vn):
        blocks = [slice(r, r + GMLP_BLOCK) for r in range(0, MIX_ROWS, GMLP_BLOCK)]
        for g in range(GMLP_GROUPS):
            w_g = jnp.where(allowed, wsp_ref[g], 0.0).astype(BF16)
            cs = slice(g * gw, (g + 1) * gw)
            mixed = _dot(w_g, jnp.concatenate([vn[rs, cs] for rs in blocks], axis=1))
            for b, rs in enumerate(blocks):
                mixed_b = mixed[:, b * gw:(b + 1) * gw] + bsp_ref[:, cs]
                out_rows = slice(rows.start + rs.start, rows.start + rs.stop)
                ya_ref[out_rows, cs] = (gu[rs, cs] * mixed_b).astype(BF16)
        t = _dot(n, w_ref[:, 3 * D_MODEL:4 * D_MODEL])
        for hs in heads:
            k_ref[rows, hs] = rope(rows, t, hs).astype(BF16)

    staged = [gate_inputs(rows) for rows in groups]
    for rows, (n, _, _) in zip(groups, staged):
        values_and_queries(rows, n)
    for rows, (n, gu, vn) in zip(groups, staged):
        spatial_gate_and_keys(rows, n, gu, vn)


def _mix_in(x2, g_pre, w_mix, ln_g, ln_b, w_sp, b_sp_full, cos, sin_signed, seq):
    tokens = x2.shape[0]
    tm = TOKEN_TILE
    tiles = tokens // tm
    tiles_per_seq = seq // tm
    tok_spec = pl.BlockSpec((tm, D_MODEL), lambda t: (t, 0))
    pos_spec = pl.BlockSpec((tm, HEAD_WIDTH), lambda t: (t % tiles_per_seq, 0))
    feat_spec = pl.BlockSpec((None, D_MODEL, tm), lambda t: (t, 0, 0))
    tok_out = jax.ShapeDtypeStruct((tokens, D_MODEL), BF16)
    feat_out = jax.ShapeDtypeStruct((tiles, D_MODEL, tm), BF16)
    return pl.pallas_call(
        _mix_in_kernel,
        grid=(tiles,),
        in_specs=[
            tok_spec,
            _resident((1, D_MODEL)),
            _resident((D_MODEL, MIX_WIDTH)),
            _resident((1, D_MODEL)),
            _resident((1, D_MODEL)),
            _resident((GMLP_GROUPS, GMLP_BLOCK, GMLP_BLOCK)),
            _resident((GMLP_BLOCK, D_MODEL)),
            pos_spec,
            pos_spec,
        ],
        out_specs=[tok_spec, feat_spec, tok_spec, feat_spec],
        out_shape=[tok_out, feat_out, tok_out, feat_out],
        compiler_params=pltpu.CompilerParams(
            dimension_semantics=("arbitrary",), vmem_limit_bytes=V7X_VMEM_LIMIT_BYTES),
        name="mix_in",
    )(x2, g_pre, w_mix, ln_g, ln_b, w_sp, b_sp_full, cos, sin_signed)


def _diff_attn_kernel(qt_ref, k_ref, vt_ref, lam_ref, g_ref, o_ref, *head_scratch):
    t = o_ref.shape[0]
    slabs = qt_ref.shape[0]
    heads = qt_ref.shape[1] // HEAD_WIDTH
    w = 2 * t
    s_refs = head_scratch[0::3]
    p_refs = head_scratch[1::3]
    acc_refs = head_scratch[2::3]
    i = pl.program_id(2)
    every_head = range(heads)

    def head_rows(g):
        return slice(g * HEAD_WIDTH, (g + 1) * HEAD_WIDTH)

    def head_lanes(g):
        return slice(g * w, (g + 1) * w)

    feat = lax.broadcasted_iota(jnp.int32, (HEAD_WIDTH, t), 0)

    def query_operand(g):
        qt = jnp.concatenate([qt_ref[r, head_rows(g), :] for r in range(slabs)], axis=1)
        zero = jnp.zeros_like(qt)
        first = (feat // (HEAD_DIM // 2)) % 2 == 0
        return jnp.concatenate([jnp.where(first, qt, zero), jnp.where(first, zero, qt)], axis=1)

    ones = jnp.ones((BF16_SUBLANES, t), BF16)

    def qk(g, qt2, j):
        s = _dot(k_ref[pl.ds(pl.multiple_of(j * t, t), t), head_rows(g)], qt2)
        s_refs[g][...] = s
        return jnp.max(s, axis=0, keepdims=True)

    def softmax_block(g, s_max, m):
        m_new = jnp.maximum(m, s_max)
        p_refs[g][...] = jnp.exp2(s_refs[g][...] - m_new).astype(BF16)
        return m_new, jnp.exp2(m - m_new)

    def value_operand(g, j):
        vt = jnp.concatenate(
            [vt_ref[slabs * j + r, head_rows(g), :] for r in range(slabs)], axis=1)
        return jnp.concatenate([vt, ones], axis=0)

    def pv(g, j, alpha):
        acc_refs[g][...] = alpha * acc_refs[g][...] + _dot(value_operand(g, j), p_refs[g][...])

    qt2 = [query_operand(g) for g in every_head]
    qry = lax.broadcasted_iota(jnp.int32, (1, w), 1) % t
    neg_inf = jnp.full((1, w), -jnp.inf, F32)

    def finish(g, m, first):
        def band(c):
            return jnp.where(qry >= c * CHUNK, s_refs[g][c * CHUNK:(c + 1) * CHUNK, :], -jnp.inf)

        m_new = m
        for c in range(t // CHUNK):
            m_new = jnp.maximum(m_new, jnp.max(band(c), axis=0, keepdims=True))
        for c in range(t // CHUNK):
            p_refs[g][c * CHUNK:(c + 1) * CHUNK, :] = jnp.exp2(band(c) - m_new).astype(BF16)
        if first:
            acc_refs[g][...] = _dot(value_operand(g, 0), p_refs[g][...])
        else:
            pv(g, i, jnp.exp2(m - m_new))

        lam_p = lam_ref[...]
        lam = (jnp.exp(jnp.sum(lam_p[0:1] * lam_p[1:2], axis=-1, keepdims=True))
               - jnp.exp(jnp.sum(lam_p[2:3] * lam_p[3:4], axis=-1, keepdims=True))
               + LAMBDA_INIT)
        inv_l = 1.0 / acc_refs[g][HEAD_WIDTH:HEAD_WIDTH + 1, :]
        o = (acc_refs[g][:HEAD_WIDTH, :t] * inv_l[:, :t]
             - lam * (acc_refs[g][:HEAD_WIDTH, t:] * inv_l[:, t:]))
        y = o * lax.rsqrt(jnp.mean(o * o, axis=0, keepdims=True) + EPS)
        o_ref[:, head_rows(g)] = (y.T * g_ref[...] * (1.0 - LAMBDA_INIT)).astype(BF16)

    @pl.when(i == 0)
    def _():
        for g in every_head:
            qk(g, qt2[g], 0)
        for g in every_head:
            finish(g, neg_inf, first=True)

    @pl.when(i > 0)
    def _():
        s_max = [qk(g, qt2[g], 0) for g in every_head]
        for g in every_head:
            acc_refs[g][...] = jnp.zeros_like(acc_refs[g])
        stats = [softmax_block(g, s_max[g], neg_inf) for g in every_head]
        s_max = [qk(g, qt2[g], 1) for g in every_head]

        def kv_step(j, carry):
            s_max, alpha, m = carry
            for g in every_head:
                pv(g, j - 1, alpha[g])
            stats = [softmax_block(g, s_max[g], m[g]) for g in every_head]
            s_max = [qk(g, qt2[g], j + 1) for g in every_head]
            return s_max, [a for _, a in stats], [mm for mm, _ in stats]

        init = (s_max, [a for _, a in stats], [mm for mm, _ in stats])
        _, alpha, m = lax.fori_loop(1, i, kv_step, init)
        for g in every_head:
            pv(g, i - 1, alpha[g])
        for g in every_head:
            finish(g, m[g], first=False)


def _diff_attn(qt, k, vt, lam_params, subln_gain):
    batch, seq, _ = k.shape
    t = ATTN_TILE
    gw = ATTN_HEADS * HEAD_WIDTH
    w = 2 * t
    slabs = t // TOKEN_TILE
    head_scratch = [
        pltpu.VMEM((t, w), F32),
        pltpu.VMEM((t, w), BF16),
        pltpu.VMEM((HEAD_WIDTH + BF16_SUBLANES, w), F32),
    ]
    return pl.pallas_call(
        _diff_attn_kernel,
        grid=(batch, HEADS // ATTN_HEADS, seq // t),
        in_specs=[
            pl.BlockSpec((None, slabs, gw, TOKEN_TILE), lambda b, h, i: (b, i, h, 0)),
            pl.BlockSpec((None, seq, gw), lambda b, h, i: (b, 0, h)),
            pl.BlockSpec((None, seq // TOKEN_TILE, gw, TOKEN_TILE), lambda b, h, i: (b, 0, h, 0)),
            pl.BlockSpec((4, HEAD_DIM), lambda b, h, i: (0, 0)),
            pl.BlockSpec((1, HEAD_WIDTH), lambda b, h, i: (0, 0)),
        ],
        out_specs=pl.BlockSpec((None, t, gw), lambda b, h, i: (b, i, h)),
        out_shape=jax.ShapeDtypeStruct((batch, seq, D_MODEL), BF16),
        scratch_shapes=head_scratch * ATTN_HEADS,
        compiler_params=pltpu.CompilerParams(
            dimension_semantics=("arbitrary", "arbitrary", "arbitrary"),
            vmem_limit_bytes=V7X_VMEM_LIMIT_BYTES),
        name="diff_attn",
    )(qt, k, vt, lam_params, subln_gain)


def _merge_ffn_kernel(x_ref, ya_ref, yb_ref, p_ref,
                      g_pre_ref, w_gate_ref, w_a_ref, w_b_ref, w_out_ref, g_post_mix_ref,
                      g_pre_ffn_ref, w_ff1_ref, w_ff2_ref, g_post_ffn_ref,
                      w_ple_proj_ref, w_ple_gate_ref, b_ple_gate_ref, g_post_ple_ref,
                      o_ref):
    tm = x_ref.shape[0]
    groups = [slice(r, r + MERGE_ROWS) for r in range(0, tm, MERGE_ROWS)]

    def out_proj(rows):
        n = _rms(x_ref[rows, :], g_pre_ref[...]).astype(BF16)
        gates = jax.nn.sigmoid(_dot(n, w_gate_ref[...]))
        merged = (gates[:, :D_MODEL] * _dot(ya_ref[rows, :], w_a_ref[...])
                  + gates[:, D_MODEL:] * _dot(yb_ref[rows, :], w_b_ref[...]))
        return _dot(merged.astype(BF16), w_out_ref[...])

    def ff_in(rows, mixed):
        h = x_ref[rows, :] + _rms(mixed, g_post_mix_ref[...])
        f = _dot(_rms(h, g_pre_ffn_ref[...]).astype(BF16), w_ff1_ref[...])
        return h, jnp.square(jnp.maximum(f, 0.0)).astype(BF16)

    def ff_out(f):
        return _dot(f, w_ff2_ref[...])

    def embed(rows, h, f):
        h = h + _rms(f, g_post_ffn_ref[...])
        e = (_dot(p_ref[rows, :].astype(BF16), w_ple_proj_ref[...])
             * jax.nn.sigmoid(_dot(h.astype(BF16), w_ple_gate_ref[...]) + b_ple_gate_ref[...]))
        o_ref[rows, :] = h + _rms(e, g_post_ple_ref[...])

    mixed = [out_proj(rows) for rows in groups]
    hf = [ff_in(rows, m) for rows, m in zip(groups, mixed)]
    f2 = [ff_out(f) for _, f in hf]
    for rows, (h, _), f in zip(groups, hf, f2):
        embed(rows, h, f)


def _merge_ffn(x2, ya, yb, p2, params):
    tokens = x2.shape[0]
    tm = MERGE_TILE
    tok_spec = pl.BlockSpec((tm, D_MODEL), lambda t: (t, 0))
    return pl.pallas_call(
        _merge_ffn_kernel,
        grid=(tokens // tm,),
        in_specs=[tok_spec, tok_spec, tok_spec, pl.BlockSpec((tm, PLE_DIM), lambda t: (t, 0))]
                 + [_resident(a.shape) for a in params],
        out_specs=tok_spec,
        out_shape=jax.ShapeDtypeStruct((tokens, D_MODEL), F32),
        compiler_params=pltpu.CompilerParams(
            dimension_semantics=("arbitrary",), vmem_limit_bytes=V7X_VMEM_LIMIT_BYTES),
        name="merge_ffn",
    )(x2, ya, yb, p2, *params)


def _rope_tables(seq):
    pos = jnp.arange(seq, dtype=F32)
    inv = 1.0 / (ROPE_THETA ** (jnp.arange(0, HEAD_DIM, 2, dtype=F32) / HEAD_DIM))
    ang = pos[:, None] * inv[None, :]
    cos = jnp.cos(ang)
    sin = jnp.sin(ang)
    cos_full = jnp.concatenate([cos, cos, cos, cos], axis=-1)
    sin_signed = jnp.concatenate([-sin, -sin, sin, sin], axis=-1)
    return cos_full, sin_signed


def _rotary_layout(w):
    rows = w.shape[0]
    w = w.reshape(rows, HEADS, 2, 2, HEAD_DIM // 2)
    return w.transpose(0, 1, 3, 2, 4).reshape(rows, HEADS * HEAD_WIDTH)


def kernel(x, p, norm_pre_mix, w_in, ln_v_gain, ln_v_bias, w_spatial, b_spatial, lambda_q1, lambda_k1, lambda_q2, lambda_k2, subln_gain, w_branch_a, w_branch_b, w_out, norm_post_mix, norm_pre_ffn, w_ff1, w_ff2, norm_post_ffn, w_ple_proj, w_ple_gate, b_ple_gate, norm_post_ple):
    batch, seq, d_model = x.shape
    depth = w_in.shape[0]
    assert d_model == D_MODEL and depth == 1
    assert seq % ATTN_TILE == 0 and ATTN_TILE % TOKEN_TILE == 0 and TOKEN_TILE % GMLP_BLOCK == 0
    tokens = batch * seq
    nt = seq // TOKEN_TILE
    x2 = x.reshape(tokens, D_MODEL)
    cos, sin_signed = _rope_tables(seq)

    def row(a):
        return a[0].reshape(1, -1).astype(F32)

    w_in_bf = w_in[0].astype(BF16)
    w_mix = jnp.concatenate([
        w_in_bf[:, :2 * D_MODEL],
        _rotary_layout(w_in_bf[:, 2 * D_MODEL:3 * D_MODEL]),
        _rotary_layout(w_in_bf[:, 3 * D_MODEL:4 * D_MODEL]),
        w_in_bf[:, 4 * D_MODEL:MIX_WIDTH]], axis=1)
    b_sp_full = jnp.repeat(b_spatial[0].T.astype(F32), D_MODEL // GMLP_GROUPS, axis=1)

    ya, qt, k, vt = _mix_in(
        x2, row(norm_pre_mix), w_mix, row(ln_v_gain), row(ln_v_bias),
        w_spatial[0].astype(F32), b_sp_full, cos, sin_signed, seq)

    lam_params = jnp.concatenate([lambda_q1, lambda_k1, lambda_q2, lambda_k2], axis=0).astype(F32)
    yb = _diff_attn(qt.reshape(batch, nt, D_MODEL, TOKEN_TILE), k.reshape(batch, seq, D_MODEL),
                    vt.reshape(batch, nt, D_MODEL, TOKEN_TILE), lam_params, row(subln_gain))

    params = (
        row(norm_pre_mix), w_in_bf[:, MIX_WIDTH:], w_branch_a[0].astype(BF16),
        w_branch_b[0].astype(BF16), w_out[0].astype(BF16), row(norm_post_mix),
        row(norm_pre_ffn), w_ff1[0].astype(BF16), w_ff2[0].astype(BF16), row(norm_post_ffn),
        w_ple_proj[0].astype(BF16), w_ple_gate[0].astype(BF16), row(b_ple_gate), row(norm_post_ple),
    )
    out = _merge_ffn(x2, ya, yb.reshape(tokens, D_MODEL), p[0].reshape(tokens, PLE_DIM), params)
    return out.reshape(batch, seq, D_MODEL)
```

```python
import math

import jax
import jax.numpy as jnp
from jax import lax
from jax.experimental import pallas as pl
from jax.experimental.pallas import tpu as pltpu

D_MODEL = 1024
CHUNK = 64
GMLP_BLOCK = 128
GMLP_GROUPS = 8
HEAD_DIM = 64
HEADS = 8
HEAD_WIDTH = 2 * HEAD_DIM
PLE_DIM = 256
ROPE_THETA = 10000.0
EPS = 1e-6
LAMBDA_INIT = 0.8 - 0.6 * math.exp(-0.3 * 0)
Q_SCALE = HEAD_DIM ** -0.5 * math.log2(math.e)

MIX_WIDTH = 5 * D_MODEL
V7X_VMEM_LIMIT_BYTES = 56 * 1024 * 1024
BF16_SUBLANES = 16

TOKEN_TILE = 512
MIX_ROWS = 256
MERGE_TILE = 512
MERGE_ROWS = 256
ATTN_TILE = 512
ATTN_HEADS = 4

F32 = jnp.float32
BF16 = jnp.bfloat16


def _rms(x, gain):
    return x * lax.rsqrt(jnp.mean(x * x, axis=-1, keepdims=True) + EPS) * gain


def _dot(a, b):
    return jnp.dot(a, b, preferred_element_type=F32)


def _resident(shape):
    zeros = (0,) * len(shape)
    return pl.BlockSpec(shape, lambda *_: zeros, pipeline_mode=pl.Buffered(1))


def _mix_in_kernel(x_ref, g_ref, w_ref, lng_ref, lnb_ref, wsp_ref, bsp_ref, cos_ref, sin_ref,
                   ya_ref, qt_ref, k_ref, vt_ref):
    tm = x_ref.shape[0]
    groups = [slice(r, r + MIX_ROWS) for r in range(0, tm, MIX_ROWS)]
    heads = [slice(h * HEAD_WIDTH, (h + 1) * HEAD_WIDTH) for h in range(HEADS)]

    def gate_inputs(rows):
        n = _rms(x_ref[rows, :], g_ref[...]).astype(BF16)
        gu = jax.nn.gelu(_dot(n, w_ref[:, 0:D_MODEL]))
        gv = jax.nn.gelu(_dot(n, w_ref[:, D_MODEL:2 * D_MODEL]))
        mu = jnp.mean(gv, axis=-1, keepdims=True)
        cen = gv - mu
        var = jnp.mean(cen * cen, axis=-1, keepdims=True)
        vn = (cen * lax.rsqrt(var + EPS) * lng_ref[...] + lnb_ref[...]).astype(BF16)
        return n, gu, vn

    def rope(rows, t, hs):
        th = t[:, hs]
        return th * cos_ref[rows, :] + pltpu.roll(th, HEAD_WIDTH // 2, 1) * sin_ref[rows, :]

    def values_and_queries(rows, n):
        va = _dot(n, w_ref[:, 4 * D_MODEL:5 * D_MODEL])
        for hs in heads:
            vt_ref[hs, rows] = va[:, hs].T.astype(BF16)
        t = _dot(n, w_ref[:, 2 * D_MODEL:3 * D_MODEL])
        for hs in heads:
            qt_ref[hs, rows] = (rope(rows, t, hs) * Q_SCALE).T.astype(BF16)

    row = lax.broadcasted_iota(jnp.int32, (GMLP_BLOCK, GMLP_BLOCK), 0)
    col = lax.broadcasted_iota(jnp.int32, (GMLP_BLOCK, GMLP_BLOCK), 1)
    allowed = (col // CHUNK) <= (row // CHUNK)
    gw = D_MODEL // GMLP_GROUPS

    def spatial_gate_and_keys(rows, n, gu, vn):
        blocks = [slice(r, r + GMLP_BLOCK) for r in range(0, MIX_ROWS, GMLP_BLOCK)]
        for g in range(GMLP_GROUPS):
            w_g = jnp.where(allowed, wsp_ref[g], 0.0).astype(BF16)
            cs = slice(g * gw, (g + 1) * gw)
            mixed = _dot(w_g, jnp.concatenate([vn[rs, cs] for rs in blocks], axis=1))
            for b, rs in enumerate(blocks):
                mixed_b = mixed[:, b * gw:(b + 1) * gw] + bsp_ref[:, cs]
                out_rows = slice(rows.start + rs.start, rows.start + rs.stop)
                ya_ref[out_rows, cs] = (gu[rs, cs] * mixed_b).astype(BF16)
        t = _dot(n, w_ref[:, 3 * D_MODEL:4 * D_MODEL])
        for hs in heads:
            k_ref[rows, hs] = rope(rows, t, hs).astype(BF16)

    staged = [gate_inputs(rows) for rows in groups]
    for rows, (n, _, _) in zip(groups, staged):
        values_and_queries(rows, n)
    for rows, (n, gu, vn) in zip(groups, staged):
        spatial_gate_and_keys(rows, n, gu, vn)


def _mix_in(x2, g_pre, w_mix, ln_g, ln_b, w_sp, b_sp_full, cos, sin_signed, seq):
    tokens = x2.shape[0]
    tm = TOKEN_TILE
    tiles = tokens // tm
    tiles_per_seq = seq // tm
    tok_spec = pl.BlockSpec((tm, D_MODEL), lambda t: (t, 0))
    pos_spec = pl.BlockSpec((tm, HEAD_WIDTH), lambda t: (t % tiles_per_seq, 0))
    feat_spec = pl.BlockSpec((None, D_MODEL, tm), lambda t: (t, 0, 0))
    tok_out = jax.ShapeDtypeStruct((tokens, D_MODEL), BF16)
    feat_out = jax.ShapeDtypeStruct((tiles, D_MODEL, tm), BF16)
    return pl.pallas_call(
        _mix_in_kernel,
        grid=(tiles,),
        in_specs=[
            tok_spec,
            _resident((1, D_MODEL)),
            _resident((D_MODEL, MIX_WIDTH)),
            _resident((1, D_MODEL)),
            _resident((1, D_MODEL)),
            _resident((GMLP_GROUPS, GMLP_BLOCK, GMLP_BLOCK)),
            _resident((GMLP_BLOCK, D_MODEL)),
            pos_spec,
            pos_spec,
        ],
        out_specs=[tok_spec, feat_spec, tok_spec, feat_spec],
        out_shape=[tok_out, feat_out, tok_out, feat_out],
        compiler_params=pltpu.CompilerParams(
            dimension_semantics=("arbitrary",), vmem_limit_bytes=V7X_VMEM_LIMIT_BYTES),
        name="mix_in",
    )(x2, g_pre, w_mix, ln_g, ln_b, w_sp, b_sp_full, cos, sin_signed)


def _diff_attn_kernel(qt_ref, k_ref, vt_ref, lam_ref, g_ref, o_ref, *head_scratch):
    t = o_ref.shape[0]
    slabs = qt_ref.shape[0]
    heads = qt_ref.shape[1] // HEAD_WIDTH
    w = 2 * t
    s_refs = head_scratch[0::3]
    p_refs = head_scratch[1::3]
    acc_refs = head_scratch[2::3]
    i = pl.program_id(2)
    every_head = range(heads)

    def head_rows(g):
        return slice(g * HEAD_WIDTH, (g + 1) * HEAD_WIDTH)

    feat = lax.broadcasted_iota(jnp.int32, (HEAD_WIDTH, t), 0)

    def query_operand(g):
        qt = jnp.concatenate([qt_ref[r, head_rows(g), :] for r in range(slabs)], axis=1)
        zero = jnp.zeros_like(qt)
        first = (feat // (HEAD_DIM // 2)) % 2 == 0
        return jnp.concatenate([jnp.where(first, qt, zero), jnp.where(first, zero, qt)], axis=1)

    ones = jnp.ones((BF16_SUBLANES, t), BF16)

    def qk(g, qt2, j):
        s = _dot(k_ref[pl.ds(pl.multiple_of(j * t, t), t), head_rows(g)], qt2)
        s_refs[g][...] = s
        return jnp.max(s, axis=0, keepdims=True)

    def softmax_block(g, s_max, m):
        m_new = jnp.maximum(m, s_max)
        p_refs[g][...] = jnp.exp2(s_refs[g][...] - m_new).astype(BF16)
        return m_new, jnp.exp2(m - m_new)

    def value_operand(g, j):
        vt = jnp.concatenate(
            [vt_ref[slabs * j + r, head_rows(g), :] for r in range(slabs)], axis=1)
        return jnp.concatenate([vt, ones], axis=0)

    def pv(g, j, alpha):
        acc_refs[g][...] = alpha * acc_refs[g][...] + _dot(value_operand(g, j), p_refs[g][...])

    qt2 = [query_operand(g) for g in every_head]
    qry = lax.broadcasted_iota(jnp.int32, (1, w), 1) % t
    neg_inf = jnp.full((1, w), -jnp.inf, F32)

    def finish(g, m, first):
        def band(c):
            return jnp.where(qry >= c * CHUNK, s_refs[g][c * CHUNK:(c + 1) * CHUNK, :], -jnp.inf)

        m_new = m
        for c in range(t // CHUNK):
            m_new = jnp.maximum(m_new, jnp.max(band(c), axis=0, keepdims=True))
        for c in range(t // CHUNK):
            p_refs[g][c * CHUNK:(c + 1) * CHUNK, :] = jnp.exp2(band(c) - m_new).astype(BF16)
        if first:
            acc_refs[g][...] = _dot(value_operand(g, 0), p_refs[g][...])
        else:
            pv(g, i, jnp.exp2(m - m_new))

        lam_p = lam_ref[...]
        lam = (jnp.exp(jnp.sum(lam_p[0:1] * lam_p[1:2], axis=-1, keepdims=True))
               - jnp.exp(jnp.sum(lam_p[2:3] * lam_p[3:4], axis=-1, keepdims=True))
               + LAMBDA_INIT)
        inv_l = 1.0 / acc_refs[g][HEAD_WIDTH:HEAD_WIDTH + 1, :]
        o = (acc_refs[g][:HEAD_WIDTH, :t] * inv_l[:, :t]
             - lam * (acc_refs[g][:HEAD_WIDTH, t:] * inv_l[:, t:]))
        y = o * lax.rsqrt(jnp.mean(o * o, axis=0, keepdims=True) + EPS)
        o_ref[:, head_rows(g)] = (y.T * g_ref[...] * (1.0 - LAMBDA_INIT)).astype(BF16)

    @pl.when(i == 0)
    def _():
        for g in every_head:
            qk(g, qt2[g], 0)
        for g in every_head:
            finish(g, neg_inf, first=True)

    @pl.when(i > 0)
    def _():
        s_max = [qk(g, qt2[g], 0) for g in every_head]
        for g in every_head:
            acc_refs[g][...] = jnp.zeros_like(acc_refs[g])
        stats = [softmax_block(g, s_max[g], neg_inf) for g in every_head]
        s_max = [qk(g, qt2[g], 1) for g in every_head]

        def kv_step(j, carry):
            s_max, alpha, m = carry
            for g in every_head:
                pv(g, j - 1, alpha[g])
            stats = [softmax_block(g, s_max[g], m[g]) for g in every_head]
            s_max = [qk(g, qt2[g], j + 1) for g in every_head]
            return s_max, [a for _, a in stats], [mm for mm, _ in stats]

        init = (s_max, [a for _, a in stats], [mm for mm, _ in stats])
        _, alpha, m = lax.fori_loop(1, i, kv_step, init)
        for g in every_head:
            pv(g, i - 1, alpha[g])
        for g in every_head:
            finish(g, m[g], first=False)


def _diff_attn(qt, k, vt, lam_params, subln_gain):
    batch, seq, _ = k.shape
    t = ATTN_TILE
    gw = ATTN_HEADS * HEAD_WIDTH
    w = 2 * t
    slabs = t // TOKEN_TILE
    head_scratch = [
        pltpu.VMEM((t, w), F32),
        pltpu.VMEM((t, w), BF16),
        pltpu.VMEM((HEAD_WIDTH + BF16_SUBLANES, w), F32),
    ]
    return pl.pallas_call(
        _diff_attn_kernel,
        grid=(batch, HEADS // ATTN_HEADS, seq // t),
        in_specs=[
            pl.BlockSpec((None, slabs, gw, TOKEN_TILE), lambda b, h, i: (b, i, h, 0)),
            pl.BlockSpec((None, seq, gw), lambda b, h, i: (b, 0, h)),
            pl.BlockSpec((None, seq // TOKEN_TILE, gw, TOKEN_TILE), lambda b, h, i: (b, 0, h, 0)),
            pl.BlockSpec((4, HEAD_DIM), lambda b, h, i: (0, 0)),
            pl.BlockSpec((1, HEAD_WIDTH), lambda b, h, i: (0, 0)),
        ],
        out_specs=pl.BlockSpec((None, t, gw), lambda b, h, i: (b, i, h)),
        out_shape=jax.ShapeDtypeStruct((batch, seq, D_MODEL), BF16),
        scratch_shapes=head_scratch * ATTN_HEADS,
        compiler_params=pltpu.CompilerParams(
            dimension_semantics=("arbitrary", "arbitrary", "arbitrary"),
            vmem_limit_bytes=V7X_VMEM_LIMIT_BYTES),
        name="diff_attn",
    )(qt, k, vt, lam_params, subln_gain)


def _merge_ffn_kernel(x_ref, ya_ref, yb_ref, p_ref,
                      g_pre_ref, w_gate_ref, w_a_ref, w_b_ref, w_out_ref, g_post_mix_ref,
                      g_pre_ffn_ref, w_ff1_ref, w_ff2_ref, g_post_ffn_ref,
                      w_ple_proj_ref, w_ple_gate_ref, b_ple_gate_ref, g_post_ple_ref,
                      o_ref):
    tm = x_ref.shape[0]
    groups = [slice(r, r + MERGE_ROWS) for r in range(0, tm, MERGE_ROWS)]

    def out_proj(rows):
        n = _rms(x_ref[rows, :], g_pre_ref[...]).astype(BF16)
        gates = jax.nn.sigmoid(_dot(n, w_gate_ref[...]))
        merged = (gates[:, :D_MODEL] * _dot(ya_ref[rows, :], w_a_ref[...])
                  + gates[:, D_MODEL:] * _dot(yb_ref[rows, :], w_b_ref[...]))
        return _dot(merged.astype(BF16), w_out_ref[...])

    def ff_in(rows, mixed):
        h = x_ref[rows, :] + _rms(mixed, g_post_mix_ref[...])
        f = _dot(_rms(h, g_pre_ffn_ref[...]).astype(BF16), w_ff1_ref[...])
        return h, jnp.square(jnp.maximum(f, 0.0)).astype(BF16)

    def ff_out(f):
        return _dot(f, w_ff2_ref[...])

    def embed(rows, h, f):
        h = h + _rms(f, g_post_ffn_ref[...])
        e = (_dot(p_ref[rows, :].astype(BF16), w_ple_proj_ref[...])
             * jax.nn.sigmoid(_dot(h.astype(BF16), w_ple_gate_ref[...]) + b_ple_gate_ref[...]))
        o_ref[rows, :] = h + _rms(e, g_post_ple_ref[...])

    mixed = [out_proj(rows) for rows in groups]
    hf = [ff_in(rows, m) for rows, m in zip(groups, mixed)]
    f2 = [ff_out(f) for _, f in hf]
    for rows, (h, _), f in zip(groups, hf, f2):
        embed(rows, h, f)


def _merge_ffn(x2, ya, yb, p2, params):
    tokens = x2.shape[0]
    tm = MERGE_TILE
    tok_spec = pl.BlockSpec((tm, D_MODEL), lambda t: (t, 0))
    return pl.pallas_call(
        _merge_ffn_kernel,
        grid=(tokens // tm,),
        in_specs=[tok_spec, tok_spec, tok_spec, pl.BlockSpec((tm, PLE_DIM), lambda t: (t, 0))]
                 + [_resident(a.shape) for a in params],
        out_specs=tok_spec,
        out_shape=jax.ShapeDtypeStruct((tokens, D_MODEL), F32),
        compiler_params=pltpu.CompilerParams(
            dimension_semantics=("arbitrary",), vmem_limit_bytes=V7X_VMEM_LIMIT_BYTES),
        name="merge_ffn",
    )(x2, ya, yb, p2, *params)


def _rope_tables(seq):
    pos = jnp.arange(seq, dtype=F32)
    inv = 1.0 / (ROPE_THETA ** (jnp.arange(0, HEAD_DIM, 2, dtype=F32) / HEAD_DIM))
    ang = pos[:, None] * inv[None, :]
    cos = jnp.cos(ang)
    sin = jnp.sin(ang)
    cos_full = jnp.concatenate([cos, cos, cos, cos], axis=-1)
    sin_signed = jnp.concatenate([-sin, -sin, sin, sin], axis=-1)
    return cos_full, sin_signed


def _rotary_layout(w):
    rows = w.shape[0]
    w = w.reshape(rows, HEADS, 2, 2, HEAD_DIM // 2)
    return w.transpose(0, 1, 3, 2, 4).reshape(rows, HEADS * HEAD_WIDTH)


def kernel(x, p, norm_pre_mix, w_in, ln_v_gain, ln_v_bias, w_spatial, b_spatial, lambda_q1, lambda_k1, lambda_q2, lambda_k2, subln_gain, w_branch_a, w_branch_b, w_out, norm_post_mix, norm_pre_ffn, w_ff1, w_ff2, norm_post_ffn, w_ple_proj, w_ple_gate, b_ple_gate, norm_post_ple):
    batch, seq, d_model = x.shape
    depth = w_in.shape[0]
    assert d_model == D_MODEL and depth == 1
    assert seq % ATTN_TILE == 0 and ATTN_TILE % TOKEN_TILE == 0 and TOKEN_TILE % GMLP_BLOCK == 0
    tokens = batch * seq
    nt = seq // TOKEN_TILE
    x2 = x.reshape(tokens, D_MODEL)
    cos, sin_signed = _rope_tables(seq)

    def row(a):
        return a[0].reshape(1, -1).astype(F32)

    w_in_bf = w_in[0].astype(BF16)
    w_mix = jnp.concatenate([
        w_in_bf[:, :2 * D_MODEL],
        _rotary_layout(w_in_bf[:, 2 * D_MODEL:3 * D_MODEL]),
        _rotary_layout(w_in_bf[:, 3 * D_MODEL:4 * D_MODEL]),
        w_in_bf[:, 4 * D_MODEL:MIX_WIDTH]], axis=1)
    b_sp_full = jnp.repeat(b_spatial[0].T.astype(F32), D_MODEL // GMLP_GROUPS, axis=1)

    ya, qt, k, vt = _mix_in(
        x2, row(norm_pre_mix), w_mix, row(ln_v_gain), row(ln_v_bias),
        w_spatial[0].astype(F32), b_sp_full, cos, sin_signed, seq)

    lam_params = jnp.concatenate([lambda_q1, lambda_k1, lambda_q2, lambda_k2], axis=0).astype(F32)
    yb = _diff_attn(qt.reshape(batch, nt, D_MODEL, TOKEN_TILE), k.reshape(batch, seq, D_MODEL),
                    vt.reshape(batch, nt, D_MODEL, TOKEN_TILE), lam_params, row(subln_gain))

    params = (
        row(norm_pre_mix), w_in_bf[:, MIX_WIDTH:], w_branch_a[0].astype(BF16),
        w_branch_b[0].astype(BF16), w_out[0].astype(BF16), row(norm_post_mix),
        row(norm_pre_ffn), w_ff1[0].astype(BF16), w_ff2[0].astype(BF16), row(norm_post_ffn),
        w_ple_proj[0].astype(BF16), w_ple_gate[0].astype(BF16), row(b_ple_gate), row(norm_post_ple),
    )
    out = _merge_ffn(x2, ya, yb.reshape(tokens, D_MODEL), p[0].reshape(tokens, PLE_DIM), params)
    return out.reshape(batch, seq, D_MODEL)
```

```python
import math

import jax
import jax.numpy as jnp
from jax import lax
from jax.experimental import pallas as pl
from jax.experimental.pallas import tpu as pltpu

D_MODEL = 1024
CHUNK = 64
GMLP_BLOCK = 128
GMLP_GROUPS = 8
HEAD_DIM = 64
HEADS = 8
HEAD_WIDTH = 2 * HEAD_DIM
PLE_DIM = 256
ROPE_THETA = 10000.0
EPS = 1e-6
LAMBDA_INIT = 0.8 - 0.6 * math.exp(-0.3 * 0)
Q_SCALE = HEAD_DIM ** -0.5 * math.log2(math.e)

MIX_WIDTH = 5 * D_MODEL
V7X_VMEM_LIMIT_BYTES = 56 * 1024 * 1024
BF16_SUBLANES = 16

TOKEN_TILE = 512
MIX_ROWS = 256
MERGE_TILE = 512
MERGE_ROWS = 256
ATTN_TILE = 512
ATTN_HEADS = 4

F32 = jnp.float32
BF16 = jnp.bfloat16


def _rms(x, gain):
    return x * lax.rsqrt(jnp.mean(x * x, axis=-1, keepdims=True) + EPS) * gain


def _dot(a, b):
    return jnp.dot(a, b, preferred_element_type=F32)


def _resident(shape):
    zeros = (0,) * len(shape)
    return pl.BlockSpec(shape, lambda *_: zeros, pipeline_mode=pl.Buffered(1))


def _mix_in_kernel(x_ref, g_ref, w_ref, lng_ref, lnb_ref, wsp_ref, bsp_ref, cos_ref, sin_ref,
                   ya_ref, qt_ref, k_ref, vt_ref):
    tm = x_ref.shape[0]
    groups = [slice(r, r + MIX_ROWS) for r in range(0, tm, MIX_ROWS)]
    heads = [slice(h * HEAD_WIDTH, (h + 1) * HEAD_WIDTH) for h in range(HEADS)]

    def gate_inputs(rows):
        n = _rms(x_ref[rows, :], g_ref[...]).astype(BF16)
        gu = jax.nn.gelu(_dot(n, w_ref[:, 0:D_MODEL]))
        gv = jax.nn.gelu(_dot(n, w_ref[:, D_MODEL:2 * D_MODEL]))
        mu = jnp.mean(gv, axis=-1, keepdims=True)
        cen = gv - mu
        var = jnp.mean(cen * cen, axis=-1, keepdims=True)
        vn = (cen * lax.rsqrt(var + EPS) * lng_ref[...] + lnb_ref[...]).astype(BF16)
        return n, gu, vn

    def rope(rows, t, hs):
        th = t[:, hs]
        return th * cos_ref[rows, :] + pltpu.roll(th, HEAD_WIDTH // 2, 1) * sin_ref[rows, :]

    def values_and_queries(rows, n):
        va = _dot(n, w_ref[:, 4 * D_MODEL:5 * D_MODEL])
        for hs in heads:
            vt_ref[hs, rows] = va[:, hs].T.astype(BF16)
        t = _dot(n, w_ref[:, 2 * D_MODEL:3 * D_MODEL])
        for hs in heads:
            qt_ref[hs, rows] = (rope(rows, t, hs) * Q_SCALE).T.astype(BF16)

    row = lax.broadcasted_iota(jnp.int32, (GMLP_BLOCK, GMLP_BLOCK), 0)
    col = lax.broadcasted_iota(jnp.int32, (GMLP_BLOCK, GMLP_BLOCK), 1)
    allowed = (col // CHUNK) <= (row // CHUNK)
    gw = D_MODEL // GMLP_GROUPS

    def keys(rows, n):
        t = _dot(n, w_ref[:, 3 * D_MODEL:4 * D_MODEL])
        for hs in heads:
            k_ref[rows, hs] = rope(rows, t, hs).astype(BF16)

    def spatial_gate(rows, gu, vn):
        blocks = [slice(r, r + GMLP_BLOCK) for r in range(0, MIX_ROWS, GMLP_BLOCK)]
        for g in range(GMLP_GROUPS):
            w_g = jnp.where(allowed, wsp_ref[g], 0.0).astype(BF16)
            cs = slice(g * gw, (g + 1) * gw)
            mixed = _dot(w_g, jnp.concatenate([vn[rs, cs] for rs in blocks], axis=1))
            for b, rs in enumerate(blocks):
                mixed_b = mixed[:, b * gw:(b + 1) * gw] + bsp_ref[:, cs]
                out_rows = slice(rows.start + rs.start, rows.start + rs.stop)
                ya_ref[out_rows, cs] = (gu[rs, cs] * mixed_b).astype(BF16)

    staged = [gate_inputs(rows) for rows in groups]
    for rows, (n, _, _) in zip(groups, staged):
        values_and_queries(rows, n)
    for rows, (n, _, _) in zip(groups, staged):
        keys(rows, n)
    for rows, (_, gu, vn) in zip(groups, staged):
        spatial_gate(rows, gu, vn)


def _mix_in(x2, g_pre, w_mix, ln_g, ln_b, w_sp, b_sp_full, cos, sin_signed, seq):
    tokens = x2.shape[0]
    tm = TOKEN_TILE
    tiles = tokens // tm
    tiles_per_seq = seq // tm
    tok_spec = pl.BlockSpec((tm, D_MODEL), lambda t: (t, 0))
    pos_spec = pl.BlockSpec((tm, HEAD_WIDTH), lambda t: (t % tiles_per_seq, 0))
    feat_spec = pl.BlockSpec((None, D_MODEL, tm), lambda t: (t, 0, 0))
    tok_out = jax.ShapeDtypeStruct((tokens, D_MODEL), BF16)
    feat_out = jax.ShapeDtypeStruct((tiles, D_MODEL, tm), BF16)
    return pl.pallas_call(
        _mix_in_kernel,
        grid=(tiles,),
        in_specs=[
            tok_spec,
            _resident((1, D_MODEL)),
            _resident((D_MODEL, MIX_WIDTH)),
            _resident((1, D_MODEL)),
            _resident((1, D_MODEL)),
            _resident((GMLP_GROUPS, GMLP_BLOCK, GMLP_BLOCK)),
            _resident((GMLP_BLOCK, D_MODEL)),
            pos_spec,
            pos_spec,
        ],
        out_specs=[tok_spec, feat_spec, tok_spec, feat_spec],
        out_shape=[tok_out, feat_out, tok_out, feat_out],
        compiler_params=pltpu.CompilerParams(
            dimension_semantics=("arbitrary",), vmem_limit_bytes=V7X_VMEM_LIMIT_BYTES),
        name="mix_in",
    )(x2, g_pre, w_mix, ln_g, ln_b, w_sp, b_sp_full, cos, sin_signed)


def _diff_attn_kernel(qt_ref, k_ref, vt_ref, lam_ref, g_ref, o_ref, *head_scratch):
    t = o_ref.shape[0]
    slabs = qt_ref.shape[0]
    heads = qt_ref.shape[1] // HEAD_WIDTH
    w = 2 * t
    s_refs = head_scratch[0::3]
    p_refs = head_scratch[1::3]
    acc_refs = head_scratch[2::3]
    i = pl.program_id(2)
    every_head = range(heads)

    def head_rows(g):
        return slice(g * HEAD_WIDTH, (g + 1) * HEAD_WIDTH)

    feat = lax.broadcasted_iota(jnp.int32, (HEAD_WIDTH, t), 0)

    def query_operand(g):
        qt = jnp.concatenate([qt_ref[r, head_rows(g), :] for r in range(slabs)], axis=1)
        zero = jnp.zeros_like(qt)
        first = (feat // (HEAD_DIM // 2)) % 2 == 0
        return jnp.concatenate([jnp.where(first, qt, zero), jnp.where(first, zero, qt)], axis=1)

    ones = jnp.ones((BF16_SUBLANES, t), BF16)

    def qk(g, qt2, j):
        s = _dot(k_ref[pl.ds(pl.multiple_of(j * t, t), t), head_rows(g)], qt2)
        s_refs[g][...] = s
        return jnp.max(s, axis=0, keepdims=True)

    def softmax_block(g, s_max, m):
        m_new = jnp.maximum(m, s_max)
        p_refs[g][...] = jnp.exp2(s_refs[g][...] - m_new).astype(BF16)
        return m_new, jnp.exp2(m - m_new)

    def value_operand(g, j):
        vt = jnp.concatenate(
            [vt_ref[slabs * j + r, head_rows(g), :] for r in range(slabs)], axis=1)
        return jnp.concatenate([vt, ones], axis=0)

    def pv(g, j, alpha):
        acc_refs[g][...] = alpha * acc_refs[g][...] + _dot(value_operand(g, j), p_refs[g][...])

    qt2 = [query_operand(g) for g in every_head]
    qry = lax.broadcasted_iota(jnp.int32, (1, w), 1) % t
    neg_inf = jnp.full((1, w), -jnp.inf, F32)

    def finish(g, m, first):
        def band(c):
            return jnp.where(qry >= c * CHUNK, s_refs[g][c * CHUNK:(c + 1) * CHUNK, :], -jnp.inf)

        m_new = m
        for c in range(t // CHUNK):
            m_new = jnp.maximum(m_new, jnp.max(band(c), axis=0, keepdims=True))
        for c in range(t // CHUNK):
            p_refs[g][c * CHUNK:(c + 1) * CHUNK, :] = jnp.exp2(band(c) - m_new).astype(BF16)
        if first:
            acc_refs[g][...] = _dot(value_operand(g, 0), p_refs[g][...])
        else:
            pv(g, i, jnp.exp2(m - m_new))

        lam_p = lam_ref[...]
        lam = (jnp.exp(jnp.sum(lam_p[0:1] * lam_p[1:2], axis=-1, keepdims=True))
               - jnp.exp(jnp.sum(lam_p[2:3] * lam_p[3:4], axis=-1, keepdims=True))
               + LAMBDA_INIT)
        inv_l = 1.0 / acc_refs[g][HEAD_WIDTH:HEAD_WIDTH + 1, :]
        o = (acc_refs[g][:HEAD_WIDTH, :t] * inv_l[:, :t]
             - lam * (acc_refs[g][:HEAD_WIDTH, t:] * inv_l[:, t:]))
        y = o * lax.rsqrt(jnp.mean(o * o, axis=0, keepdims=True) + EPS)
        o_ref[:, head_rows(g)] = (y.T * g_ref[...] * (1.0 - LAMBDA_INIT)).astype(BF16)

    @pl.when(i == 0)
    def _():
        for g in every_head:
            qk(g, qt2[g], 0)
        for g in every_head:
            finish(g, neg_inf, first=True)

    @pl.when(i > 0)
    def _():
        s_max = [qk(g, qt2[g], 0) for g in every_head]
        for g in every_head:
            acc_refs[g][...] = jnp.zeros_like(acc_refs[g])
        stats = [softmax_block(g, s_max[g], neg_inf) for g in every_head]
        s_max = [qk(g, qt2[g], 1) for g in every_head]

        def kv_step(j, carry):
            s_max, alpha, m = carry
            for g in every_head:
                pv(g, j - 1, alpha[g])
            stats = [softmax_block(g, s_max[g], m[g]) for g in every_head]
            s_max = [qk(g, qt2[g], j + 1) for g in every_head]
            return s_max, [a for _, a in stats], [mm for mm, _ in stats]

        init = (s_max, [a for _, a in stats], [mm for mm, _ in stats])
        _, alpha, m = lax.fori_loop(1, i, kv_step, init)
        for g in every_head:
            pv(g, i - 1, alpha[g])
        for g in every_head:
            finish(g, m[g], first=False)


def _diff_attn(qt, k, vt, lam_params, subln_gain):
    batch, seq, _ = k.shape
    t = ATTN_TILE
    gw = ATTN_HEADS * HEAD_WIDTH
    w = 2 * t
    slabs = t // TOKEN_TILE
    head_scratch = [
        pltpu.VMEM((t, w), F32),
        pltpu.VMEM((t, w), BF16),
        pltpu.VMEM((HEAD_WIDTH + BF16_SUBLANES, w), F32),
    ]
    return pl.pallas_call(
        _diff_attn_kernel,
        grid=(batch, HEADS // ATTN_HEADS, seq // t),
        in_specs=[
            pl.BlockSpec((None, slabs, gw, TOKEN_TILE), lambda b, h, i: (b, i, h, 0)),
            pl.BlockSpec((None, seq, gw), lambda b, h, i: (b, 0, h)),
            pl.BlockSpec((None, seq // TOKEN_TILE, gw, TOKEN_TILE), lambda b, h, i: (b, 0, h, 0)),
            pl.BlockSpec((4, HEAD_DIM), lambda b, h, i: (0, 0)),
            pl.BlockSpec((1, HEAD_WIDTH), lambda b, h, i: (0, 0)),
        ],
        out_specs=pl.BlockSpec((None, t, gw), lambda b, h, i: (b, i, h)),
        out_shape=jax.ShapeDtypeStruct((batch, seq, D_MODEL), BF16),
        scratch_shapes=head_scratch * ATTN_HEADS,
        compiler_params=pltpu.CompilerParams(
            dimension_semantics=("arbitrary", "arbitrary", "arbitrary"),
            vmem_limit_bytes=V7X_VMEM_LIMIT_BYTES),
        name="diff_attn",
    )(qt, k, vt, lam_params, subln_gain)


def _merge_ffn_kernel(x_ref, ya_ref, yb_ref, p_ref,
                      g_pre_ref, w_gate_ref, w_a_ref, w_b_ref, w_out_ref, g_post_mix_ref,
                      g_pre_ffn_ref, w_ff1_ref, w_ff2_ref, g_post_ffn_ref,
                      w_ple_proj_ref, w_ple_gate_ref, b_ple_gate_ref, g_post_ple_ref,
                      o_ref):
    tm = x_ref.shape[0]
    groups = [slice(r, r + MERGE_ROWS) for r in range(0, tm, MERGE_ROWS)]

    def out_proj(rows):
        n = _rms(x_ref[rows, :], g_pre_ref[...]).astype(BF16)
        gates = jax.nn.sigmoid(_dot(n, w_gate_ref[...]))
        merged = (gates[:, :D_MODEL] * _dot(ya_ref[rows, :], w_a_ref[...])
                  + gates[:, D_MODEL:] * _dot(yb_ref[rows, :], w_b_ref[...]))
        return _dot(merged.astype(BF16), w_out_ref[...])

    def ff_in(rows, mixed):
        h = x_ref[rows, :] + _rms(mixed, g_post_mix_ref[...])
        f = _dot(_rms(h, g_pre_ffn_ref[...]).astype(BF16), w_ff1_ref[...])
        return h, jnp.square(jnp.maximum(f, 0.0)).astype(BF16)

    def ff_out(f):
        return _dot(f, w_ff2_ref[...])

    def embed(rows, h, f):
        h = h + _rms(f, g_post_ffn_ref[...])
        e = (_dot(p_ref[rows, :].astype(BF16), w_ple_proj_ref[...])
             * jax.nn.sigmoid(_dot(h.astype(BF16), w_ple_gate_ref[...]) + b_ple_gate_ref[...]))
        o_ref[rows, :] = h + _rms(e, g_post_ple_ref[...])

    mixed = [out_proj(rows) for rows in groups]
    hf = [ff_in(rows, m) for rows, m in zip(groups, mixed)]
    f2 = [ff_out(f) for _, f in hf]
    for rows, (h, _), f in zip(groups, hf, f2):
        embed(rows, h, f)


def _merge_ffn(x2, ya, yb, p2, params):
    tokens = x2.shape[0]
    tm = MERGE_TILE
    tok_spec = pl.BlockSpec((tm, D_MODEL), lambda t: (t, 0))
    return pl.pallas_call(
        _merge_ffn_kernel,
        grid=(tokens // tm,),
        in_specs=[tok_spec, tok_spec, tok_spec, pl.BlockSpec((tm, PLE_DIM), lambda t: (t, 0))]
                 + [_resident(a.shape) for a in params],
        out_specs=tok_spec,
        out_shape=jax.ShapeDtypeStruct((tokens, D_MODEL), F32),
        compiler_params=pltpu.CompilerParams(
            dimension_semantics=("arbitrary",), vmem_limit_bytes=V7X_VMEM_LIMIT_BYTES),
        name="merge_ffn",
    )(x2, ya, yb, p2, *params)


def _rope_tables(seq):
    pos = jnp.arange(seq, dtype=F32)
    inv = 1.0 / (ROPE_THETA ** (jnp.arange(0, HEAD_DIM, 2, dtype=F32) / HEAD_DIM))
    ang = pos[:, None] * inv[None, :]
    cos = jnp.cos(ang)
    sin = jnp.sin(ang)
    cos_full = jnp.concatenate([cos, cos, cos, cos], axis=-1)
    sin_signed = jnp.concatenate([-sin, -sin, sin, sin], axis=-1)
    return cos_full, sin_signed


def _rotary_layout(w):
    rows = w.shape[0]
    w = w.reshape(rows, HEADS, 2, 2, HEAD_DIM // 2)
    return w.transpose(0, 1, 3, 2, 4).reshape(rows, HEADS * HEAD_WIDTH)


def kernel(x, p, norm_pre_mix, w_in, ln_v_gain, ln_v_bias, w_spatial, b_spatial, lambda_q1, lambda_k1, lambda_q2, lambda_k2, subln_gain, w_branch_a, w_branch_b, w_out, norm_post_mix, norm_pre_ffn, w_ff1, w_ff2, norm_post_ffn, w_ple_proj, w_ple_gate, b_ple_gate, norm_post_ple):
    batch, seq, d_model = x.shape
    depth = w_in.shape[0]
    assert d_model == D_MODEL and depth == 1
    assert seq % ATTN_TILE == 0 and ATTN_TILE % TOKEN_TILE == 0 and TOKEN_TILE % GMLP_BLOCK == 0
    tokens = batch * seq
    nt = seq // TOKEN_TILE
    x2 = x.reshape(tokens, D_MODEL)
    cos, sin_signed = _rope_tables(seq)

    def row(a):
        return a[0].reshape(1, -1).astype(F32)

    w_in_bf = w_in[0].astype(BF16)
    w_mix = jnp.concatenate([
        w_in_bf[:, :2 * D_MODEL],
        _rotary_layout(w_in_bf[:, 2 * D_MODEL:3 * D_MODEL]),
        _rotary_layout(w_in_bf[:, 3 * D_MODEL:4 * D_MODEL]),
        w_in_bf[:, 4 * D_MODEL:MIX_WIDTH]], axis=1)
    b_sp_full = jnp.repeat(b_spatial[0].T.astype(F32), D_MODEL // GMLP_GROUPS, axis=1)

    ya, qt, k, vt = _mix_in(
        x2, row(norm_pre_mix), w_mix, row(ln_v_gain), row(ln_v_bias),
        w_spatial[0].astype(F32), b_sp_full, cos, sin_signed, seq)

    lam_params = jnp.concatenate([lambda_q1, lambda_k1, lambda_q2, lambda_k2], axis=0).astype(F32)
    yb = _diff_attn(qt.reshape(batch, nt, D_MODEL, TOKEN_TILE), k.reshape(batch, seq, D_MODEL),
                    vt.reshape(batch, nt, D_MODEL, TOKEN_TILE), lam_params, row(subln_gain))

    params = (
        row(norm_pre_mix), w_in_bf[:, MIX_WIDTH:], w_branch_a[0].astype(BF16),
        w_branch_b[0].astype(BF16), w_out[0].astype(BF16), row(norm_post_mix),
        row(norm_pre_ffn), w_ff1[0].astype(BF16), w_ff2[0].astype(BF16), row(norm_post_ffn),
        w_ple_proj[0].astype(BF16), w_ple_gate[0].astype(BF16), row(b_ple_gate), row(norm_post_ple),
    )
    out = _merge_ffn(x2, ya, yb.reshape(tokens, D_MODEL), p[0].reshape(tokens, PLE_DIM), params)
    return out.reshape(batch, seq, D_MODEL)
```

```python
import math

import jax
import jax.numpy as jnp
from jax import lax
from jax.experimental import pallas as pl
from jax.experimental.pallas import tpu as pltpu

D_MODEL = 1024
CHUNK = 64
GMLP_BLOCK = 128
GMLP_GROUPS = 8
HEAD_DIM = 64
HEADS = 8
HEAD_WIDTH = 2 * HEAD_DIM
PLE_DIM = 256
ROPE_THETA = 10000.0
EPS = 1e-6
LAMBDA_INIT = 0.8 - 0.6 * math.exp(-0.3 * 0)
Q_SCALE = HEAD_DIM ** -0.5 * math.log2(math.e)

MIX_WIDTH = 5 * D_MODEL
V7X_VMEM_LIMIT_BYTES = 56 * 1024 * 1024
BF16_SUBLANES = 16

TOKEN_TILE = 512
MIX_ROWS = 256
MERGE_TILE = 512
MERGE_ROWS = 256
ATTN_TILE = 512
ATTN_HEADS = 4

F32 = jnp.float32
BF16 = jnp.bfloat16


def _rms(x, gain):
    return x * lax.rsqrt(jnp.mean(x * x, axis=-1, keepdims=True) + EPS) * gain


def _dot(a, b):
    return jnp.dot(a, b, preferred_element_type=F32)


def _resident(shape):
    zeros = (0,) * len(shape)
    return pl.BlockSpec(shape, lambda *_: zeros, pipeline_mode=pl.Buffered(1))


def _mix_in_kernel(x_ref, g_ref, w_ref, lng_ref, lnb_ref, wsp_ref, bsp_ref, cos_ref, sin_ref,
                   ya_ref, qt_ref, k_ref, vt_ref):
    tm = x_ref.shape[0]
    groups = [slice(r, r + MIX_ROWS) for r in range(0, tm, MIX_ROWS)]
    heads = [slice(h * HEAD_WIDTH, (h + 1) * HEAD_WIDTH) for h in range(HEADS)]

    def gate_inputs(rows):
        n = _rms(x_ref[rows, :], g_ref[...]).astype(BF16)
        gu = jax.nn.gelu(_dot(n, w_ref[:, 0:D_MODEL]))
        gv = jax.nn.gelu(_dot(n, w_ref[:, D_MODEL:2 * D_MODEL]))
        mu = jnp.mean(gv, axis=-1, keepdims=True)
        cen = gv - mu
        var = jnp.mean(cen * cen, axis=-1, keepdims=True)
        vn = (cen * lax.rsqrt(var + EPS) * lng_ref[...] + lnb_ref[...]).astype(BF16)
        return n, gu, vn

    def rope(rows, t, hs):
        th = t[:, hs]
        return th * cos_ref[rows, :] + pltpu.roll(th, HEAD_WIDTH // 2, 1) * sin_ref[rows, :]

    def values_and_queries(rows, n):
        va = _dot(n, w_ref[:, 4 * D_MODEL:5 * D_MODEL])
        for hs in heads:
            vt_ref[hs, rows] = va[:, hs].T.astype(BF16)
        t = _dot(n, w_ref[:, 2 * D_MODEL:3 * D_MODEL])
        for hs in heads:
            qt_ref[hs, rows] = (rope(rows, t, hs) * Q_SCALE).T.astype(BF16)

    row = lax.broadcasted_iota(jnp.int32, (GMLP_BLOCK, GMLP_BLOCK), 0)
    col = lax.broadcasted_iota(jnp.int32, (GMLP_BLOCK, GMLP_BLOCK), 1)
    allowed = (col // CHUNK) <= (row // CHUNK)
    gw = D_MODEL // GMLP_GROUPS

    def spatial_gate_and_keys(rows, n, gu, vn):
        blocks = [slice(r, r + GMLP_BLOCK) for r in range(0, MIX_ROWS, GMLP_BLOCK)]
        for g in range(GMLP_GROUPS):
            w_g = jnp.where(allowed, wsp_ref[g], 0.0).astype(BF16)
            cs = slice(g * gw, (g + 1) * gw)
            mixed = _dot(w_g, jnp.concatenate([vn[rs, cs] for rs in blocks], axis=1))
            for b, rs in enumerate(blocks):
                mixed_b = mixed[:, b * gw:(b + 1) * gw] + bsp_ref[:, cs]
                out_rows = slice(rows.start + rs.start, rows.start + rs.stop)
                ya_ref[out_rows, cs] = (gu[rs, cs] * mixed_b).astype(BF16)
        t = _dot(n, w_ref[:, 3 * D_MODEL:4 * D_MODEL])
        for hs in heads:
            k_ref[rows, hs] = rope(rows, t, hs).astype(BF16)

    staged = [gate_inputs(rows) for rows in groups]
    for rows, (n, _, _) in zip(groups, staged):
        values_and_queries(rows, n)
    for rows, (n, gu, vn) in zip(groups, staged):
        spatial_gate_and_keys(rows, n, gu, vn)


def _mix_in(x2, g_pre, w_mix, ln_g, ln_b, w_sp, b_sp_full, cos, sin_signed, seq):
    tokens = x2.shape[0]
    tm = TOKEN_TILE
    tiles = tokens // tm
    tiles_per_seq = seq // tm
    tok_spec = pl.BlockSpec((tm, D_MODEL), lambda t: (t, 0))
    pos_spec = pl.BlockSpec((tm, HEAD_WIDTH), lambda t: (t % tiles_per_seq, 0))
    feat_spec = pl.BlockSpec((None, D_MODEL, tm), lambda t: (t, 0, 0))
    tok_out = jax.ShapeDtypeStruct((tokens, D_MODEL), BF16)
    feat_out = jax.ShapeDtypeStruct((tiles, D_MODEL, tm), BF16)
    return pl.pallas_call(
        _mix_in_kernel,
        grid=(tiles,),
        in_specs=[
            tok_spec,
            _resident((1, D_MODEL)),
            _resident((D_MODEL, MIX_WIDTH)),
            _resident((1, D_MODEL)),
            _resident((1, D_MODEL)),
            _resident((GMLP_GROUPS, GMLP_BLOCK, GMLP_BLOCK)),
            _resident((GMLP_BLOCK, D_MODEL)),
            pos_spec,
            pos_spec,
        ],
        out_specs=[tok_spec, feat_spec, tok_spec, feat_spec],
        out_shape=[tok_out, feat_out, tok_out, feat_out],
        compiler_params=pltpu.CompilerParams(
            dimension_semantics=("arbitrary",), vmem_limit_bytes=V7X_VMEM_LIMIT_BYTES),
        name="mix_in",
    )(x2, g_pre, w_mix, ln_g, ln_b, w_sp, b_sp_full, cos, sin_signed)


def _diff_attn_kernel(qt_ref, k_ref, vt_ref, lam_ref, g_ref, o_ref, *head_scratch):
    t = o_ref.shape[0]
    slabs = qt_ref.shape[0]
    heads = qt_ref.shape[1] // HEAD_WIDTH
    w = 2 * t
    s_refs = head_scratch[0::3]
    p_refs = head_scratch[1::3]
    acc_refs = head_scratch[2::3]
    i = pl.program_id(2)
    every_head = range(heads)

    def head_rows(g):
        return slice(g * HEAD_WIDTH, (g + 1) * HEAD_WIDTH)

    feat = lax.broadcasted_iota(jnp.int32, (HEAD_WIDTH, t), 0)

    def query_operand(g):
        qt = jnp.concatenate([qt_ref[r, head_rows(g), :] for r in range(slabs)], axis=1)
        zero = jnp.zeros_like(qt)
        first = (feat // (HEAD_DIM // 2)) % 2 == 0
        return jnp.concatenate([jnp.where(first, qt, zero), jnp.where(first, zero, qt)], axis=1)

    ones = jnp.ones((BF16_SUBLANES, t), BF16)

    def qk(g, qt2, j):
        s = _dot(k_ref[pl.ds(pl.multiple_of(j * t, t), t), head_rows(g)], qt2)
        s_refs[g][...] = s
        return jnp.max(s, axis=0, keepdims=True)

    def softmax_block(g, s_max, m, after=None):
        m_new = jnp.maximum(m, s_max)
        if after is not None:
            m_new = m_new + exact_zero(after)
        p_refs[g][...] = jnp.exp2(s_refs[g][...] - m_new).astype(BF16)
        return m_new, jnp.exp2(m - m_new)

    def value_operand(g, j):
        vt = jnp.concatenate(
            [vt_ref[slabs * j + r, head_rows(g), :] for r in range(slabs)], axis=1)
        return jnp.concatenate([vt, ones], axis=0)

    def pv(g, j, alpha):
        acc = alpha * acc_refs[g][...] + _dot(value_operand(g, j), p_refs[g][...])
        acc_refs[g][...] = acc
        return acc[HEAD_WIDTH:HEAD_WIDTH + 1, :]

    def exact_zero(x):
        bits = lax.bitcast_convert_type(x, jnp.uint32)
        return ((bits >> 16) >> 16).astype(F32)

    qt2 = [query_operand(g) for g in every_head]
    qry = lax.broadcasted_iota(jnp.int32, (1, w), 1) % t
    neg_inf = jnp.full((1, w), -jnp.inf, F32)

    def finish(g, m, first):
        def band(c):
            return jnp.where(qry >= c * CHUNK, s_refs[g][c * CHUNK:(c + 1) * CHUNK, :], -jnp.inf)

        m_new = m
        for c in range(t // CHUNK):
            m_new = jnp.maximum(m_new, jnp.max(band(c), axis=0, keepdims=True))
        for c in range(t // CHUNK):
            p_refs[g][c * CHUNK:(c + 1) * CHUNK, :] = jnp.exp2(band(c) - m_new).astype(BF16)
        if first:
            acc_refs[g][...] = _dot(value_operand(g, 0), p_refs[g][...])
        else:
            pv(g, i, jnp.exp2(m - m_new))

        lam_p = lam_ref[...]
        lam = (jnp.exp(jnp.sum(lam_p[0:1] * lam_p[1:2], axis=-1, keepdims=True))
               - jnp.exp(jnp.sum(lam_p[2:3] * lam_p[3:4], axis=-1, keepdims=True))
               + LAMBDA_INIT)
        inv_l = 1.0 / acc_refs[g][HEAD_WIDTH:HEAD_WIDTH + 1, :]
        o = (acc_refs[g][:HEAD_WIDTH, :t] * inv_l[:, :t]
             - lam * (acc_refs[g][:HEAD_WIDTH, t:] * inv_l[:, t:]))
        y = o * lax.rsqrt(jnp.mean(o * o, axis=0, keepdims=True) + EPS)
        o_ref[:, head_rows(g)] = (y.T * g_ref[...] * (1.0 - LAMBDA_INIT)).astype(BF16)

    @pl.when(i == 0)
    def _():
        for g in every_head:
            qk(g, qt2[g], 0)
        for g in every_head:
            finish(g, neg_inf, first=True)

    @pl.when(i > 0)
    def _():
        s_max = [qk(g, qt2[g], 0) for g in every_head]
        for g in every_head:
            acc_refs[g][...] = jnp.zeros_like(acc_refs[g])
        stats = [softmax_block(g, s_max[g], neg_inf) for g in every_head]
        s_max = [qk(g, qt2[g], 1) for g in every_head]

        def kv_step(j, carry):
            s_max, alpha, m = carry
            done = [pv(g, j - 1, alpha[g]) for g in every_head]
            stats = [softmax_block(g, s_max[g], m[g], done[heads - 1]) for g in every_head]
            s_max = [qk(g, qt2[g], j + 1) for g in every_head]
            return s_max, [a for _, a in stats], [mm for mm, _ in stats]

        init = (s_max, [a for _, a in stats], [mm for mm, _ in stats])
        _, alpha, m = lax.fori_loop(1, i, kv_step, init)
        for g in every_head:
            pv(g, i - 1, alpha[g])
        for g in every_head:
            finish(g, m[g], first=False)


def _diff_attn(qt, k, vt, lam_params, subln_gain):
    batch, seq, _ = k.shape
    t = ATTN_TILE
    gw = ATTN_HEADS * HEAD_WIDTH
    w = 2 * t
    slabs = t // TOKEN_TILE
    head_scratch = [
        pltpu.VMEM((t, w), F32),
        pltpu.VMEM((t, w), BF16),
        pltpu.VMEM((HEAD_WIDTH + BF16_SUBLANES, w), F32),
    ]
    return pl.pallas_call(
        _diff_attn_kernel,
        grid=(batch, HEADS // ATTN_HEADS, seq // t),
        in_specs=[
            pl.BlockSpec((None, slabs, gw, TOKEN_TILE), lambda b, h, i: (b, i, h, 0)),
            pl.BlockSpec((None, seq, gw), lambda b, h, i: (b, 0, h)),
            pl.BlockSpec((None, seq // TOKEN_TILE, gw, TOKEN_TILE), lambda b, h, i: (b, 0, h, 0)),
            pl.BlockSpec((4, HEAD_DIM), lambda b, h, i: (0, 0)),
            pl.BlockSpec((1, HEAD_WIDTH), lambda b, h, i: (0, 0)),
        ],
        out_specs=pl.BlockSpec((None, t, gw), lambda b, h, i: (b, i, h)),
        out_shape=jax.ShapeDtypeStruct((batch, seq, D_MODEL), BF16),
        scratch_shapes=head_scratch * ATTN_HEADS,
        compiler_params=pltpu.CompilerParams(
            dimension_semantics=("arbitrary", "arbitrary", "arbitrary"),
            vmem_limit_bytes=V7X_VMEM_LIMIT_BYTES),
        name="diff_attn",
    )(qt, k, vt, lam_params, subln_gain)


def _merge_ffn_kernel(x_ref, ya_ref, yb_ref, p_ref,
                      g_pre_ref, w_gate_ref, w_a_ref, w_b_ref, w_out_ref, g_post_mix_ref,
                      g_pre_ffn_ref, w_ff1_ref, w_ff2_ref, g_post_ffn_ref,
                      w_ple_proj_ref, w_ple_gate_ref, b_ple_gate_ref, g_post_ple_ref,
                      o_ref):
    tm = x_ref.shape[0]
    groups = [slice(r, r + MERGE_ROWS) for r in range(0, tm, MERGE_ROWS)]

    def out_proj(rows):
        n = _rms(x_ref[rows, :], g_pre_ref[...]).astype(BF16)
        gates = jax.nn.sigmoid(_dot(n, w_gate_ref[...]))
        merged = (gates[:, :D_MODEL] * _dot(ya_ref[rows, :], w_a_ref[...])
                  + gates[:, D_MODEL:] * _dot(yb_ref[rows, :], w_b_ref[...]))
        return _dot(merged.astype(BF16), w_out_ref[...])

    def ff_in(rows, mixed):
        h = x_ref[rows, :] + _rms(mixed, g_post_mix_ref[...])
        f = _dot(_rms(h, g_pre_ffn_ref[...]).astype(BF16), w_ff1_ref[...])
        return h, jnp.square(jnp.maximum(f, 0.0)).astype(BF16)

    def ff_out(f):
        return _dot(f, w_ff2_ref[...])

    def embed(rows, h, f):
        h = h + _rms(f, g_post_ffn_ref[...])
        e = (_dot(p_ref[rows, :].astype(BF16), w_ple_proj_ref[...])
             * jax.nn.sigmoid(_dot(h.astype(BF16), w_ple_gate_ref[...]) + b_ple_gate_ref[...]))
        o_ref[rows, :] = h + _rms(e, g_post_ple_ref[...])

    mixed = [out_proj(rows) for rows in groups]
    hf = [ff_in(rows, m) for rows, m in zip(groups, mixed)]
    f2 = [ff_out(f) for _, f in hf]
    for rows, (h, _), f in zip(groups, hf, f2):
        embed(rows, h, f)


def _merge_ffn(x2, ya, yb, p2, params):
    tokens = x2.shape[0]
    tm = MERGE_TILE
    tok_spec = pl.BlockSpec((tm, D_MODEL), lambda t: (t, 0))
    return pl.pallas_call(
        _merge_ffn_kernel,
        grid=(tokens // tm,),
        in_specs=[tok_spec, tok_spec, tok_spec, pl.BlockSpec((tm, PLE_DIM), lambda t: (t, 0))]
                 + [_resident(a.shape) for a in params],
        out_specs=tok_spec,
        out_shape=jax.ShapeDtypeStruct((tokens, D_MODEL), F32),
        compiler_params=pltpu.CompilerParams(
            dimension_semantics=("arbitrary",), vmem_limit_bytes=V7X_VMEM_LIMIT_BYTES),
        name="merge_ffn",
    )(x2, ya, yb, p2, *params)


def _rope_tables(seq):
    pos = jnp.arange(seq, dtype=F32)
    inv = 1.0 / (ROPE_THETA ** (jnp.arange(0, HEAD_DIM, 2, dtype=F32) / HEAD_DIM))
    ang = pos[:, None] * inv[None, :]
    cos = jnp.cos(ang)
    sin = jnp.sin(ang)
    cos_full = jnp.concatenate([cos, cos, cos, cos], axis=-1)
    sin_signed = jnp.concatenate([-sin, -sin, sin, sin], axis=-1)
    return cos_full, sin_signed


def _rotary_layout(w):
    rows = w.shape[0]
    w = w.reshape(rows, HEADS, 2, 2, HEAD_DIM // 2)
    return w.transpose(0, 1, 3, 2, 4).reshape(rows, HEADS * HEAD_WIDTH)


def kernel(x, p, norm_pre_mix, w_in, ln_v_gain, ln_v_bias, w_spatial, b_spatial, lambda_q1, lambda_k1, lambda_q2, lambda_k2, subln_gain, w_branch_a, w_branch_b, w_out, norm_post_mix, norm_pre_ffn, w_ff1, w_ff2, norm_post_ffn, w_ple_proj, w_ple_gate, b_ple_gate, norm_post_ple):
    batch, seq, d_model = x.shape
    depth = w_in.shape[0]
    assert d_model == D_MODEL and depth == 1
    assert seq % ATTN_TILE == 0 and ATTN_TILE % TOKEN_TILE == 0 and TOKEN_TILE % GMLP_BLOCK == 0
    tokens = batch * seq
    nt = seq // TOKEN_TILE
    x2 = x.reshape(tokens, D_MODEL)
    cos, sin_signed = _rope_tables(seq)

    def row(a):
        return a[0].reshape(1, -1).astype(F32)

    w_in_bf = w_in[0].astype(BF16)
    w_mix = jnp.concatenate([
        w_in_bf[:, :2 * D_MODEL],
        _rotary_layout(w_in_bf[:, 2 * D_MODEL:3 * D_MODEL]),
        _rotary_layout(w_in_bf[:, 3 * D_MODEL:4 * D_MODEL]),
        w_in_bf[:, 4 * D_MODEL:MIX_WIDTH]], axis=1)
    b_sp_full = jnp.repeat(b_spatial[0].T.astype(F32), D_MODEL // GMLP_GROUPS, axis=1)

    ya, qt, k, vt = _mix_in(
        x2, row(norm_pre_mix), w_mix, row(ln_v_gain), row(ln_v_bias),
        w_spatial[0].astype(F32), b_sp_full, cos, sin_signed, seq)

    lam_params = jnp.concatenate([lambda_q1, lambda_k1, lambda_q2, lambda_k2], axis=0).astype(F32)
    yb = _diff_attn(qt.reshape(batch, nt, D_MODEL, TOKEN_TILE), k.reshape(batch, seq, D_MODEL),
                    vt.reshape(batch, nt, D_MODEL, TOKEN_TILE), lam_params, row(subln_gain))

    params = (
        row(norm_pre_mix), w_in_bf[:, MIX_WIDTH:], w_branch_a[0].astype(BF16),
        w_branch_b[0].astype(BF16), w_out[0].astype(BF16), row(norm_post_mix),
        row(norm_pre_ffn), w_ff1[0].astype(BF16), w_ff2[0].astype(BF16), row(norm_post_ffn),
        w_ple_proj[0].astype(BF16), w_ple_gate[0].astype(BF16), row(b_ple_gate), row(norm_post_ple),
    )
    out = _merge_ffn(x2, ya, yb.reshape(tokens, D_MODEL), p[0].reshape(tokens, PLE_DIM), params)
    return out.reshape(batch, seq, D_MODEL)
```

```python
import math

import jax
import jax.numpy as jnp
from jax import lax
from jax.experimental import pallas as pl
from jax.experimental.pallas import tpu as pltpu

D_MODEL = 1024
CHUNK = 64
GMLP_BLOCK = 128
GMLP_GROUPS = 8
HEAD_DIM = 64
HEADS = 8
HEAD_WIDTH = 2 * HEAD_DIM
PLE_DIM = 256
ROPE_THETA = 10000.0
EPS = 1e-6
LAMBDA_INIT = 0.8 - 0.6 * math.exp(-0.3 * 0)
Q_SCALE = HEAD_DIM ** -0.5 * math.log2(math.e)

MIX_WIDTH = 5 * D_MODEL
V7X_VMEM_LIMIT_BYTES = 60 * 1024 * 1024
BF16_SUBLANES = 16

TOKEN_TILE = 512
MIX_ROWS = 256
MERGE_TILE = 512
MERGE_ROWS = 256
ATTN_TILE = 512
ATTN_HEADS = 4

F32 = jnp.float32
BF16 = jnp.bfloat16


def _rms(x, gain):
    return x * lax.rsqrt(jnp.mean(x * x, axis=-1, keepdims=True) + EPS) * gain


def _dot(a, b):
    return jnp.dot(a, b, preferred_element_type=F32)


def _resident(shape):
    zeros = (0,) * len(shape)
    return pl.BlockSpec(shape, lambda *_: zeros, pipeline_mode=pl.Buffered(1))


def _mix_in_kernel(x_ref, g_ref, w_ref, lng_ref, lnb_ref, wsp_ref, bsp_ref, cos_ref, sin_ref,
                   ya_ref, qt_ref, k_ref, vt_ref):
    tm = x_ref.shape[0]
    groups = [slice(r, r + MIX_ROWS) for r in range(0, tm, MIX_ROWS)]
    heads = [slice(h * HEAD_WIDTH, (h + 1) * HEAD_WIDTH) for h in range(HEADS)]

    def gate_inputs(rows):
        n = _rms(x_ref[rows, :], g_ref[...]).astype(BF16)
        gu = jax.nn.gelu(_dot(n, w_ref[:, 0:D_MODEL]))
        gv = jax.nn.gelu(_dot(n, w_ref[:, D_MODEL:2 * D_MODEL]))
        mu = jnp.mean(gv, axis=-1, keepdims=True)
        cen = gv - mu
        var = jnp.mean(cen * cen, axis=-1, keepdims=True)
        vn = (cen * lax.rsqrt(var + EPS) * lng_ref[...] + lnb_ref[...]).astype(BF16)
        return n, gu, vn

    def rope(rows, t, hs):
        th = t[:, hs]
        return th * cos_ref[rows, :] + pltpu.roll(th, HEAD_WIDTH // 2, 1) * sin_ref[rows, :]

    def values_and_queries(rows, n):
        va = _dot(n, w_ref[:, 4 * D_MODEL:5 * D_MODEL])
        for hs in heads:
            vt_ref[hs, rows] = va[:, hs].T.astype(BF16)
        t = _dot(n, w_ref[:, 2 * D_MODEL:3 * D_MODEL])
        for hs in heads:
            qt_ref[hs, rows] = (rope(rows, t, hs) * Q_SCALE).T.astype(BF16)

    row = lax.broadcasted_iota(jnp.int32, (GMLP_BLOCK, GMLP_BLOCK), 0)
    col = lax.broadcasted_iota(jnp.int32, (GMLP_BLOCK, GMLP_BLOCK), 1)
    allowed = (col // CHUNK) <= (row // CHUNK)
    gw = D_MODEL // GMLP_GROUPS

    def keys(rows, n):
        t = _dot(n, w_ref[:, 3 * D_MODEL:4 * D_MODEL])
        for hs in heads:
            k_ref[rows, hs] = rope(rows, t, hs).astype(BF16)

    def spatial_gate(rows, gu, vn):
        blocks = [slice(r, r + GMLP_BLOCK) for r in range(0, MIX_ROWS, GMLP_BLOCK)]
        for g in range(GMLP_GROUPS):
            w_g = jnp.where(allowed, wsp_ref[g], 0.0).astype(BF16)
            cs = slice(g * gw, (g + 1) * gw)
            mixed = _dot(w_g, jnp.concatenate([vn[rs, cs] for rs in blocks], axis=1))
            for b, rs in enumerate(blocks):
                mixed_b = mixed[:, b * gw:(b + 1) * gw] + bsp_ref[:, cs]
                out_rows = slice(rows.start + rs.start, rows.start + rs.stop)
                ya_ref[out_rows, cs] = (gu[rs, cs] * mixed_b).astype(BF16)

    staged = [gate_inputs(rows) for rows in groups]
    for rows, (n, _, _) in zip(groups, staged):
        values_and_queries(rows, n)
    for rows, (n, _, _) in zip(groups, staged):
        keys(rows, n)
    for rows, (_, gu, vn) in zip(groups, staged):
        spatial_gate(rows, gu, vn)


def _mix_in(x2, g_pre, w_mix, ln_g, ln_b, w_sp, b_sp_full, cos, sin_signed, seq):
    tokens = x2.shape[0]
    tm = TOKEN_TILE
    tiles = tokens // tm
    tiles_per_seq = seq // tm
    tok_spec = pl.BlockSpec((tm, D_MODEL), lambda t: (t, 0))
    pos_spec = pl.BlockSpec((tm, HEAD_WIDTH), lambda t: (t % tiles_per_seq, 0))
    feat_spec = pl.BlockSpec((None, D_MODEL, tm), lambda t: (t, 0, 0))
    tok_out = jax.ShapeDtypeStruct((tokens, D_MODEL), BF16)
    feat_out = jax.ShapeDtypeStruct((tiles, D_MODEL, tm), BF16)
    return pl.pallas_call(
        _mix_in_kernel,
        grid=(tiles,),
        in_specs=[
            tok_spec,
            _resident((1, D_MODEL)),
            _resident((D_MODEL, MIX_WIDTH)),
            _resident((1, D_MODEL)),
            _resident((1, D_MODEL)),
            _resident((GMLP_GROUPS, GMLP_BLOCK, GMLP_BLOCK)),
            _resident((GMLP_BLOCK, D_MODEL)),
            pos_spec,
            pos_spec,
        ],
        out_specs=[tok_spec, feat_spec, tok_spec, feat_spec],
        out_shape=[tok_out, feat_out, tok_out, feat_out],
        compiler_params=pltpu.CompilerParams(
            dimension_semantics=("arbitrary",), vmem_limit_bytes=V7X_VMEM_LIMIT_BYTES),
        name="mix_in",
    )(x2, g_pre, w_mix, ln_g, ln_b, w_sp, b_sp_full, cos, sin_signed)


def _diff_attn_kernel(qt_ref, k_ref, vt_ref, lam_ref, g_ref, o_ref, *head_scratch):
    t = o_ref.shape[0]
    slabs = qt_ref.shape[0]
    heads = qt_ref.shape[1] // HEAD_WIDTH
    w = 2 * t
    s_refs = head_scratch[0::3]
    p_refs = head_scratch[1::3]
    acc_refs = head_scratch[2::3]
    i = pl.program_id(2)
    every_head = range(heads)

    def head_rows(g):
        return slice(g * HEAD_WIDTH, (g + 1) * HEAD_WIDTH)

    feat = lax.broadcasted_iota(jnp.int32, (HEAD_WIDTH, t), 0)

    def query_operand(g):
        qt = jnp.concatenate([qt_ref[r, head_rows(g), :] for r in range(slabs)], axis=1)
        zero = jnp.zeros_like(qt)
        first = (feat // (HEAD_DIM // 2)) % 2 == 0
        return jnp.concatenate([jnp.where(first, qt, zero), jnp.where(first, zero, qt)], axis=1)

    ones = jnp.ones((BF16_SUBLANES, t), BF16)

    def qk(g, qt2, j):
        s = _dot(k_ref[pl.ds(pl.multiple_of(j * t, t), t), head_rows(g)], qt2)
        s_refs[g][...] = s
        return jnp.max(s, axis=0, keepdims=True)

    def softmax_block(g, s_max, m):
        m_new = jnp.maximum(m, s_max)
        p_refs[g][...] = jnp.exp2(s_refs[g][...] - m_new).astype(BF16)
        return m_new, jnp.exp2(m - m_new)

    def value_operand(g, j):
        vt = jnp.concatenate(
            [vt_ref[slabs * j + r, head_rows(g), :] for r in range(slabs)], axis=1)
        return jnp.concatenate([vt, ones], axis=0)

    def pv(g, j, alpha):
        acc_refs[g][...] = alpha * acc_refs[g][...] + _dot(value_operand(g, j), p_refs[g][...])

    qt2 = [query_operand(g) for g in every_head]
    qry = lax.broadcasted_iota(jnp.int32, (1, w), 1) % t
    neg_inf = jnp.full((1, w), -jnp.inf, F32)
    half = t // 2
    low = [slice(0, half), slice(t, t + half)]
    high = [slice(half, t), slice(t + half, w)]

    def finish(g, m, first):
        bands = range(t // CHUNK)

        def lanes_of(c):
            return low + high if c * CHUNK < half else high

        def band(c, cols):
            s = s_refs[g][c * CHUNK:(c + 1) * CHUNK, cols]
            if c * CHUNK > cols.start % t:
                s = jnp.where(qry[:, cols] >= c * CHUNK, s, -jnp.inf)
            return s

        m_cols = {cols.start: m[:, cols] for cols in low + high}
        for c in bands:
            for cols in lanes_of(c):
                m_cols[cols.start] = jnp.maximum(
                    m_cols[cols.start], jnp.max(band(c, cols), axis=0, keepdims=True))
        for c in bands:
            for cols in lanes_of(c):
                p_refs[g][c * CHUNK:(c + 1) * CHUNK, cols] = (
                    jnp.exp2(band(c, cols) - m_cols[cols.start]).astype(BF16))
        m_new = jnp.concatenate([m_cols[k] for k in sorted(m_cols)], axis=1)

        vb = value_operand(g, 0 if first else i)
        upper = _dot(vb[:, :half], p_refs[g][:half, :])
        if first:
            acc_refs[g][...] = upper
        else:
            acc_refs[g][...] = jnp.exp2(m - m_new) * acc_refs[g][...] + upper
        for cols in high:
            acc_refs[g][:, cols] += _dot(vb[:, half:], p_refs[g][half:, cols])

        lam_p = lam_ref[...]
        lam = (jnp.exp(jnp.sum(lam_p[0:1] * lam_p[1:2], axis=-1, keepdims=True))
               - jnp.exp(jnp.sum(lam_p[2:3] * lam_p[3:4], axis=-1, keepdims=True))
               + LAMBDA_INIT)
        inv_l = 1.0 / acc_refs[g][HEAD_WIDTH:HEAD_WIDTH + 1, :]
        o = (acc_refs[g][:HEAD_WIDTH, :t] * inv_l[:, :t]
             - lam * (acc_refs[g][:HEAD_WIDTH, t:] * inv_l[:, t:]))
        y = o * lax.rsqrt(jnp.mean(o * o, axis=0, keepdims=True) + EPS)
        o_ref[:, head_rows(g)] = (y.T * g_ref[...] * (1.0 - LAMBDA_INIT)).astype(BF16)

    def qk_diag(g):
        base = pl.multiple_of(i * t, t)
        s_refs[g][:half, :] = _dot(k_ref[pl.ds(base, half), head_rows(g)], qt2[g])
        lower_keys = k_ref[pl.ds(pl.multiple_of(base + half, half), half), head_rows(g)]
        for cols in high:
            s_refs[g][half:, cols] = _dot(lower_keys, qt2[g][:, cols])

    def first_stage():
        s_max = [qk(g, qt2[g], 0) for g in every_head]
        stats = [softmax_block(g, s_max[g], neg_inf) for g in every_head]
        return [a for _, a in stats], [mm for mm, _ in stats]

    @pl.when(i == 0)
    def _():
        for g in every_head:
            qk_diag(g)
        for g in every_head:
            finish(g, neg_inf, first=True)

    @pl.when(i == 1)
    def _():
        _, m = first_stage()
        for g in every_head:
            qk_diag(g)
        for g in every_head:
            acc_refs[g][...] = _dot(value_operand(g, 0), p_refs[g][...])
        for g in every_head:
            finish(g, m[g], first=False)

    @pl.when(i > 1)
    def _():
        for g in every_head:
            acc_refs[g][...] = jnp.zeros_like(acc_refs[g])
        alpha, m = first_stage()
        s_max = [qk(g, qt2[g], 1) for g in every_head]

        def kv_stage(j, s_max, alpha, m, next_scores):
            for g in every_head:
                pv(g, j - 1, alpha[g])
            stats = [softmax_block(g, s_max[g], m[g]) for g in every_head]
            s_max = [next_scores(g) for g in every_head]
            return s_max, [a for _, a in stats], [mm for mm, _ in stats]

        def kv_step(j, carry):
            return kv_stage(j, *carry, lambda g: qk(g, qt2[g], j + 1))

        carry = lax.fori_loop(1, i - 1, kv_step, (s_max, alpha, m))
        _, alpha, m = kv_stage(i - 1, *carry, qk_diag)
        for g in every_head:
            pv(g, i - 1, alpha[g])
        for g in every_head:
            finish(g, m[g], first=False)


def _diff_attn(qt, k, vt, lam_params, subln_gain):
    batch, seq, _ = k.shape
    t = ATTN_TILE
    gw = ATTN_HEADS * HEAD_WIDTH
    w = 2 * t
    slabs = t // TOKEN_TILE
    head_scratch = [
        pltpu.VMEM((t, w), F32),
        pltpu.VMEM((t, w), BF16),
        pltpu.VMEM((HEAD_WIDTH + BF16_SUBLANES, w), F32),
    ]
    return pl.pallas_call(
        _diff_attn_kernel,
        grid=(batch, HEADS // ATTN_HEADS, seq // t),
        in_specs=[
            pl.BlockSpec((None, slabs, gw, TOKEN_TILE), lambda b, h, i: (b, i, h, 0)),
            pl.BlockSpec((None, seq, gw), lambda b, h, i: (b, 0, h)),
            pl.BlockSpec((None, seq // TOKEN_TILE, gw, TOKEN_TILE), lambda b, h, i: (b, 0, h, 0)),
            pl.BlockSpec((4, HEAD_DIM), lambda b, h, i: (0, 0)),
            pl.BlockSpec((1, HEAD_WIDTH), lambda b, h, i: (0, 0)),
        ],
        out_specs=pl.BlockSpec((None, t, gw), lambda b, h, i: (b, i, h)),
        out_shape=jax.ShapeDtypeStruct((batch, seq, D_MODEL), BF16),
        scratch_shapes=head_scratch * ATTN_HEADS,
        compiler_params=pltpu.CompilerParams(
            dimension_semantics=("arbitrary", "arbitrary", "arbitrary"),
            vmem_limit_bytes=V7X_VMEM_LIMIT_BYTES),
        name="diff_attn",
    )(qt, k, vt, lam_params, subln_gain)


def _merge_ffn_kernel(x_ref, ya_ref, yb_ref, p_ref,
                      g_pre_ref, w_gate_ref, w_a_ref, w_b_ref, w_out_ref, g_post_mix_ref,
                      g_pre_ffn_ref, w_ff1_ref, w_ff2_ref, g_post_ffn_ref,
                      w_ple_proj_ref, w_ple_gate_ref, b_ple_gate_ref, g_post_ple_ref,
                      o_ref):
    tm = x_ref.shape[0]
    groups = [slice(r, r + MERGE_ROWS) for r in range(0, tm, MERGE_ROWS)]

    def out_proj(rows):
        n = _rms(x_ref[rows, :], g_pre_ref[...]).astype(BF16)
        gates = jax.nn.sigmoid(_dot(n, w_gate_ref[...]))
        merged = (gates[:, :D_MODEL] * _dot(ya_ref[rows, :], w_a_ref[...])
                  + gates[:, D_MODEL:] * _dot(yb_ref[rows, :], w_b_ref[...]))
        return _dot(merged.astype(BF16), w_out_ref[...])

    def ff_in(rows, mixed):
        h = x_ref[rows, :] + _rms(mixed, g_post_mix_ref[...])
        f = _dot(_rms(h, g_pre_ffn_ref[...]).astype(BF16), w_ff1_ref[...])
        return h, jnp.square(jnp.maximum(f, 0.0)).astype(BF16)

    def ff_out(f):
        return _dot(f, w_ff2_ref[...])

    def embed(rows, h, f):
        h = h + _rms(f, g_post_ffn_ref[...])
        e = (_dot(p_ref[rows, :].astype(BF16), w_ple_proj_ref[...])
             * jax.nn.sigmoid(_dot(h.astype(BF16), w_ple_gate_ref[...]) + b_ple_gate_ref[...]))
        o_ref[rows, :] = h + _rms(e, g_post_ple_ref[...])

    mixed = [out_proj(rows) for rows in groups]
    hf = [ff_in(rows, m) for rows, m in zip(groups, mixed)]
    f2 = [ff_out(f) for _, f in hf]
    for rows, (h, _), f in zip(groups, hf, f2):
        embed(rows, h, f)


def _merge_ffn(x2, ya, yb, p2, params):
    tokens = x2.shape[0]
    tm = MERGE_TILE
    tok_spec = pl.BlockSpec((tm, D_MODEL), lambda t: (t, 0))
    return pl.pallas_call(
        _merge_ffn_kernel,
        grid=(tokens // tm,),
        in_specs=[tok_spec, tok_spec, tok_spec, pl.BlockSpec((tm, PLE_DIM), lambda t: (t, 0))]
                 + [_resident(a.shape) for a in params],
        out_specs=tok_spec,
        out_shape=jax.ShapeDtypeStruct((tokens, D_MODEL), F32),
        compiler_params=pltpu.CompilerParams(
            dimension_semantics=("arbitrary",), vmem_limit_bytes=V7X_VMEM_LIMIT_BYTES),
        name="merge_ffn",
    )(x2, ya, yb, p2, *params)


def _rope_tables(seq):
    pos = jnp.arange(seq, dtype=F32)
    inv = 1.0 / (ROPE_THETA ** (jnp.arange(0, HEAD_DIM, 2, dtype=F32) / HEAD_DIM))
    ang = pos[:, None] * inv[None, :]
    cos = jnp.cos(ang)
    sin = jnp.sin(ang)
    cos_full = jnp.concatenate([cos, cos, cos, cos], axis=-1)
    sin_signed = jnp.concatenate([-sin, -sin, sin, sin], axis=-1)
    return cos_full, sin_signed


def _rotary_layout(w):
    rows = w.shape[0]
    w = w.reshape(rows, HEADS, 2, 2, HEAD_DIM // 2)
    return w.transpose(0, 1, 3, 2, 4).reshape(rows, HEADS * HEAD_WIDTH)


def kernel(x, p, norm_pre_mix, w_in, ln_v_gain, ln_v_bias, w_spatial, b_spatial, lambda_q1, lambda_k1, lambda_q2, lambda_k2, subln_gain, w_branch_a, w_branch_b, w_out, norm_post_mix, norm_pre_ffn, w_ff1, w_ff2, norm_post_ffn, w_ple_proj, w_ple_gate, b_ple_gate, norm_post_ple):
    batch, seq, d_model = x.shape
    depth = w_in.shape[0]
    assert d_model == D_MODEL and depth == 1
    assert seq % ATTN_TILE == 0 and ATTN_TILE % TOKEN_TILE == 0 and TOKEN_TILE % GMLP_BLOCK == 0
    tokens = batch * seq
    nt = seq // TOKEN_TILE
    x2 = x.reshape(tokens, D_MODEL)
    cos, sin_signed = _rope_tables(seq)

    def row(a):
        return a[0].reshape(1, -1).astype(F32)

    w_in_bf = w_in[0].astype(BF16)
    w_mix = jnp.concatenate([
        w_in_bf[:, :2 * D_MODEL],
        _rotary_layout(w_in_bf[:, 2 * D_MODEL:3 * D_MODEL]),
        _rotary_layout(w_in_bf[:, 3 * D_MODEL:4 * D_MODEL]),
        w_in_bf[:, 4 * D_MODEL:MIX_WIDTH]], axis=1)
    b_sp_full = jnp.repeat(b_spatial[0].T.astype(F32), D_MODEL // GMLP_GROUPS, axis=1)

    ya, qt, k, vt = _mix_in(
        x2, row(norm_pre_mix), w_mix, row(ln_v_gain), row(ln_v_bias),
        w_spatial[0].astype(F32), b_sp_full, cos, sin_signed, seq)

    lam_params = jnp.concatenate([lambda_q1, lambda_k1, lambda_q2, lambda_k2], axis=0).astype(F32)
    yb = _diff_attn(qt.reshape(batch, nt, D_MODEL, TOKEN_TILE), k.reshape(batch, seq, D_MODEL),
                    vt.reshape(batch, nt, D_MODEL, TOKEN_TILE), lam_params, row(subln_gain))

    params = (
        row(norm_pre_mix), w_in_bf[:, MIX_WIDTH:], w_branch_a[0].astype(BF16),
        w_branch_b[0].astype(BF16), w_out[0].astype(BF16), row(norm_post_mix),
        row(norm_pre_ffn), w_ff1[0].astype(BF16), w_ff2[0].astype(BF16), row(norm_post_ffn),
        w_ple_proj[0].astype(BF16), w_ple_gate[0].astype(BF16), row(b_ple_gate), row(norm_post_ple),
    )
    out = _merge_ffn(x2, ya, yb.reshape(tokens, D_MODEL), p[0].reshape(tokens, PLE_DIM), params)
    return out.reshape(batch, seq, D_MODEL)
```

```python
import math

import jax
import jax.numpy as jnp
from jax import lax
from jax.experimental import pallas as pl
from jax.experimental.pallas import tpu as pltpu

D_MODEL = 1024
CHUNK = 64
GMLP_BLOCK = 128
GMLP_GROUPS = 8
HEAD_DIM = 64
HEADS = 8
HEAD_WIDTH = 2 * HEAD_DIM
PLE_DIM = 256
ROPE_THETA = 10000.0
EPS = 1e-6
LAMBDA_INIT = 0.8 - 0.6 * math.exp(-0.3 * 0)
Q_SCALE = HEAD_DIM ** -0.5 * math.log2(math.e)

MIX_WIDTH = 5 * D_MODEL
V7X_VMEM_LIMIT_BYTES = 60 * 1024 * 1024
BF16_SUBLANES = 16

TOKEN_TILE = 512
MIX_ROWS = 256
MERGE_TILE = 512
MERGE_ROWS = 256
ATTN_TILE = 512
ATTN_HEADS = 4

F32 = jnp.float32
BF16 = jnp.bfloat16


def _rms(x, gain):
    return x * lax.rsqrt(jnp.mean(x * x, axis=-1, keepdims=True) + EPS) * gain


def _dot(a, b):
    return jnp.dot(a, b, preferred_element_type=F32)


def _resident(shape):
    zeros = (0,) * len(shape)
    return pl.BlockSpec(shape, lambda *_: zeros, pipeline_mode=pl.Buffered(1))


def _mix_in_kernel(x_ref, g_ref, w_ref, lng_ref, lnb_ref, wsp_ref, bsp_ref, cos_ref, sin_ref,
                   ya_ref, qt_ref, k_ref, vt_ref):
    tm = x_ref.shape[0]
    groups = [slice(r, r + MIX_ROWS) for r in range(0, tm, MIX_ROWS)]
    heads = [slice(h * HEAD_WIDTH, (h + 1) * HEAD_WIDTH) for h in range(HEADS)]

    def gate_inputs(rows):
        n = _rms(x_ref[rows, :], g_ref[...]).astype(BF16)
        gu = jax.nn.gelu(_dot(n, w_ref[:, 0:D_MODEL]))
        gv = jax.nn.gelu(_dot(n, w_ref[:, D_MODEL:2 * D_MODEL]))
        mu = jnp.mean(gv, axis=-1, keepdims=True)
        cen = gv - mu
        var = jnp.mean(cen * cen, axis=-1, keepdims=True)
        vn = (cen * lax.rsqrt(var + EPS) * lng_ref[...] + lnb_ref[...]).astype(BF16)
        return n, gu, vn

    def rope(rows, t, hs):
        th = t[:, hs]
        return th * cos_ref[rows, :] + pltpu.roll(th, HEAD_WIDTH // 2, 1) * sin_ref[rows, :]

    def values_and_queries(rows, n):
        va = _dot(n, w_ref[:, 4 * D_MODEL:5 * D_MODEL])
        for hs in heads:
            vt_ref[hs, rows] = va[:, hs].T.astype(BF16)
        t = _dot(n, w_ref[:, 2 * D_MODEL:3 * D_MODEL])
        for hs in heads:
            qt_ref[hs, rows] = (rope(rows, t, hs) * Q_SCALE).T.astype(BF16)

    row = lax.broadcasted_iota(jnp.int32, (GMLP_BLOCK, GMLP_BLOCK), 0)
    col = lax.broadcasted_iota(jnp.int32, (GMLP_BLOCK, GMLP_BLOCK), 1)
    allowed = (col // CHUNK) <= (row // CHUNK)
    gw = D_MODEL // GMLP_GROUPS

    def keys(rows, n):
        t = _dot(n, w_ref[:, 3 * D_MODEL:4 * D_MODEL])
        for hs in heads:
            k_ref[rows, hs] = rope(rows, t, hs).astype(BF16)

    def spatial_gate(rows, gu, vn):
        blocks = [slice(r, r + GMLP_BLOCK) for r in range(0, MIX_ROWS, GMLP_BLOCK)]
        for g in range(GMLP_GROUPS):
            w_g = jnp.where(allowed, wsp_ref[g], 0.0).astype(BF16)
            cs = slice(g * gw, (g + 1) * gw)
            mixed = _dot(w_g, jnp.concatenate([vn[rs, cs] for rs in blocks], axis=1))
            for b, rs in enumerate(blocks):
                mixed_b = mixed[:, b * gw:(b + 1) * gw] + bsp_ref[:, cs]
                out_rows = slice(rows.start + rs.start, rows.start + rs.stop)
                ya_ref[out_rows, cs] = (gu[rs, cs] * mixed_b).astype(BF16)

    staged = [gate_inputs(rows) for rows in groups]
    for rows, (n, _, _) in zip(groups, staged):
        values_and_queries(rows, n)
    for rows, (n, _, _) in zip(groups, staged):
        keys(rows, n)
    for rows, (_, gu, vn) in zip(groups, staged):
        spatial_gate(rows, gu, vn)


def _mix_in(x2, g_pre, w_mix, ln_g, ln_b, w_sp, b_sp_full, cos, sin_signed, seq):
    tokens = x2.shape[0]
    tm = TOKEN_TILE
    tiles = tokens // tm
    tiles_per_seq = seq // tm
    tok_spec = pl.BlockSpec((tm, D_MODEL), lambda t: (t, 0))
    pos_spec = pl.BlockSpec((tm, HEAD_WIDTH), lambda t: (t % tiles_per_seq, 0))
    feat_spec = pl.BlockSpec((None, D_MODEL, tm), lambda t: (t, 0, 0))
    tok_out = jax.ShapeDtypeStruct((tokens, D_MODEL), BF16)
    feat_out = jax.ShapeDtypeStruct((tiles, D_MODEL, tm), BF16)
    return pl.pallas_call(
        _mix_in_kernel,
        grid=(tiles,),
        in_specs=[
            tok_spec,
            _resident((1, D_MODEL)),
            _resident((D_MODEL, MIX_WIDTH)),
            _resident((1, D_MODEL)),
            _resident((1, D_MODEL)),
            _resident((GMLP_GROUPS, GMLP_BLOCK, GMLP_BLOCK)),
            _resident((GMLP_BLOCK, D_MODEL)),
            pos_spec,
            pos_spec,
        ],
        out_specs=[tok_spec, feat_spec, tok_spec, feat_spec],
        out_shape=[tok_out, feat_out, tok_out, feat_out],
        compiler_params=pltpu.CompilerParams(
            dimension_semantics=("arbitrary",), vmem_limit_bytes=V7X_VMEM_LIMIT_BYTES),
        name="mix_in",
    )(x2, g_pre, w_mix, ln_g, ln_b, w_sp, b_sp_full, cos, sin_signed)


def _diff_attn_kernel(qt_ref, k_ref, vt_ref, lam_ref, g_ref, o_ref, *head_scratch):
    t = o_ref.shape[0]
    slabs = qt_ref.shape[0]
    heads = qt_ref.shape[1] // HEAD_WIDTH
    w = 2 * t
    s_refs = head_scratch[0::3]
    p_refs = head_scratch[1::3]
    acc_refs = head_scratch[2::3]
    i = pl.program_id(2)
    every_head = range(heads)

    def head_rows(g):
        return slice(g * HEAD_WIDTH, (g + 1) * HEAD_WIDTH)

    feat = lax.broadcasted_iota(jnp.int32, (HEAD_WIDTH, t), 0)

    def query_operand(g):
        qt = jnp.concatenate([qt_ref[r, head_rows(g), :] for r in range(slabs)], axis=1)
        zero = jnp.zeros_like(qt)
        first = (feat // (HEAD_DIM // 2)) % 2 == 0
        return jnp.concatenate([jnp.where(first, qt, zero), jnp.where(first, zero, qt)], axis=1)

    ones = jnp.ones((BF16_SUBLANES, t), BF16)

    def qk(g, qt2, j):
        s = _dot(k_ref[pl.ds(pl.multiple_of(j * t, t), t), head_rows(g)], qt2)
        s_refs[g][...] = s
        return jnp.max(s, axis=0, keepdims=True)

    def softmax_block(g, s_max, m):
        m_new = jnp.maximum(m, s_max)
        p_refs[g][...] = jnp.exp2(s_refs[g][...] - m_new).astype(BF16)
        return m_new, jnp.exp2(m - m_new)

    def value_operand(g, j):
        vt = jnp.concatenate(
            [vt_ref[slabs * j + r, head_rows(g), :] for r in range(slabs)], axis=1)
        return jnp.concatenate([vt, ones], axis=0)

    def pv(g, j, alpha):
        acc_refs[g][...] = alpha * acc_refs[g][...] + _dot(value_operand(g, j), p_refs[g][...])

    qt2 = [query_operand(g) for g in every_head]
    qry = lax.broadcasted_iota(jnp.int32, (1, w), 1) % t
    neg_inf = jnp.full((1, w), -jnp.inf, F32)
    half = t // 2
    low = [slice(0, half), slice(t, t + half)]
    high = [slice(half, t), slice(t + half, w)]

    def finish(g, m, first):
        bands = range(t // CHUNK)

        def lanes_of(c):
            return low + high if c * CHUNK < half else high

        def band(c, cols):
            s = s_refs[g][c * CHUNK:(c + 1) * CHUNK, cols]
            if c * CHUNK > cols.start % t:
                s = jnp.where(qry[:, cols] >= c * CHUNK, s, -jnp.inf)
            return s

        m_cols = {cols.start: m[:, cols] for cols in low + high}
        for c in bands:
            for cols in lanes_of(c):
                m_cols[cols.start] = jnp.maximum(
                    m_cols[cols.start], jnp.max(band(c, cols), axis=0, keepdims=True))
        for c in bands:
            for cols in lanes_of(c):
                p_refs[g][c * CHUNK:(c + 1) * CHUNK, cols] = (
                    jnp.exp2(band(c, cols) - m_cols[cols.start]).astype(BF16))
        m_new = jnp.concatenate([m_cols[k] for k in sorted(m_cols)], axis=1)

        vb = value_operand(g, 0 if first else i)
        upper = _dot(vb[:, :half], p_refs[g][:half, :])
        if first:
            acc_refs[g][...] = upper
        else:
            acc_refs[g][...] = jnp.exp2(m - m_new) * acc_refs[g][...] + upper
        for cols in high:
            acc_refs[g][:, cols] += _dot(vb[:, half:], p_refs[g][half:, cols])

        lam_p = lam_ref[...]
        lam = (jnp.exp(jnp.sum(lam_p[0:1] * lam_p[1:2], axis=-1, keepdims=True))
               - jnp.exp(jnp.sum(lam_p[2:3] * lam_p[3:4], axis=-1, keepdims=True))
               + LAMBDA_INIT)
        inv_l = 1.0 / acc_refs[g][HEAD_WIDTH:HEAD_WIDTH + 1, :]
        o = (acc_refs[g][:HEAD_WIDTH, :t] * inv_l[:, :t]
             - lam * (acc_refs[g][:HEAD_WIDTH, t:] * inv_l[:, t:]))
        y = o * lax.rsqrt(jnp.mean(o * o, axis=0, keepdims=True) + EPS)
        o_ref[:, head_rows(g)] = (y.T * g_ref[...] * (1.0 - LAMBDA_INIT)).astype(BF16)

    def qk_diag(g):
        base = pl.multiple_of(i * t, t)
        s_refs[g][:half, :] = _dot(k_ref[pl.ds(base, half), head_rows(g)], qt2[g])
        lower_keys = k_ref[pl.ds(pl.multiple_of(base + half, half), half), head_rows(g)]
        for cols in high:
            s_refs[g][half:, cols] = _dot(lower_keys, qt2[g][:, cols])

    def first_stage():
        s_max = [qk(g, qt2[g], 0) for g in every_head]
        stats = [softmax_block(g, s_max[g], neg_inf) for g in every_head]
        return [a for _, a in stats], [mm for mm, _ in stats]

    @pl.when(i == 0)
    def _():
        for g in every_head:
            qk_diag(g)
        for g in every_head:
            finish(g, neg_inf, first=True)

    @pl.when(i == 1)
    def _():
        _, m = first_stage()
        for g in every_head:
            qk_diag(g)
        for g in every_head:
            acc_refs[g][...] = _dot(value_operand(g, 0), p_refs[g][...])
        for g in every_head:
            finish(g, m[g], first=False)

    @pl.when(i > 1)
    def _():
        for g in every_head:
            acc_refs[g][...] = jnp.zeros_like(acc_refs[g])
        alpha, m = first_stage()
        s_max = [qk(g, qt2[g], 1) for g in every_head]

        def kv_stage(j, s_max, alpha, m, next_scores):
            for g in every_head:
                pv(g, j - 1, alpha[g])
            stats = [softmax_block(g, s_max[g], m[g]) for g in every_head]
            s_max = [next_scores(g) for g in every_head]
            return s_max, [a for _, a in stats], [mm for mm, _ in stats]

        def kv_step(j, carry):
            return kv_stage(j, *carry, lambda g: qk(g, qt2[g], j + 1))

        carry = lax.fori_loop(1, i - 1, kv_step, (s_max, alpha, m))
        _, alpha, m = kv_stage(i - 1, *carry, qk_diag)
        for g in every_head:
            pv(g, i - 1, alpha[g])
        for g in every_head:
            finish(g, m[g], first=False)


def _diff_attn(qt, k, vt, lam_params, subln_gain):
    batch, seq, _ = k.shape
    t = ATTN_TILE
    gw = ATTN_HEADS * HEAD_WIDTH
    w = 2 * t
    slabs = t // TOKEN_TILE
    head_scratch = [
        pltpu.VMEM((t, w), F32),
        pltpu.VMEM((t, w), BF16),
        pltpu.VMEM((HEAD_WIDTH + BF16_SUBLANES, w), F32),
    ]
    return pl.pallas_call(
        _diff_attn_kernel,
        grid=(batch, HEADS // ATTN_HEADS, seq // t),
        in_specs=[
            pl.BlockSpec((None, slabs, gw, TOKEN_TILE), lambda b, h, i: (b, i, h, 0)),
            pl.BlockSpec((None, seq, gw), lambda b, h, i: (b, 0, h), pipeline_mode=pl.Buffered(1)),
            pl.BlockSpec((None, seq // TOKEN_TILE, gw, TOKEN_TILE), lambda b, h, i: (b, 0, h, 0),
                         pipeline_mode=pl.Buffered(1)),
            pl.BlockSpec((4, HEAD_DIM), lambda b, h, i: (0, 0)),
            pl.BlockSpec((1, HEAD_WIDTH), lambda b, h, i: (0, 0)),
        ],
        out_specs=pl.BlockSpec((None, t, gw), lambda b, h, i: (b, i, h)),
        out_shape=jax.ShapeDtypeStruct((batch, seq, D_MODEL), BF16),
        scratch_shapes=head_scratch * ATTN_HEADS,
        compiler_params=pltpu.CompilerParams(
            dimension_semantics=("arbitrary", "arbitrary", "arbitrary"),
            vmem_limit_bytes=V7X_VMEM_LIMIT_BYTES),
        name="diff_attn",
    )(qt, k, vt, lam_params, subln_gain)


def _merge_ffn_kernel(x_ref, ya_ref, yb_ref, p_ref,
                      g_pre_ref, w_gate_ref, w_a_ref, w_b_ref, w_out_ref, g_post_mix_ref,
                      g_pre_ffn_ref, w_ff1_ref, w_ff2_ref, g_post_ffn_ref,
                      w_ple_proj_ref, w_ple_gate_ref, b_ple_gate_ref, g_post_ple_ref,
                      o_ref):
    tm = x_ref.shape[0]
    groups = [slice(r, r + MERGE_ROWS) for r in range(0, tm, MERGE_ROWS)]

    def out_proj(rows):
        n = _rms(x_ref[rows, :], g_pre_ref[...]).astype(BF16)
        gates = jax.nn.sigmoid(_dot(n, w_gate_ref[...]))
        merged = (gates[:, :D_MODEL] * _dot(ya_ref[rows, :], w_a_ref[...])
                  + gates[:, D_MODEL:] * _dot(yb_ref[rows, :], w_b_ref[...]))
        return _dot(merged.astype(BF16), w_out_ref[...])

    def ff_in(rows, mixed):
        h = x_ref[rows, :] + _rms(mixed, g_post_mix_ref[...])
        f = _dot(_rms(h, g_pre_ffn_ref[...]).astype(BF16), w_ff1_ref[...])
        return h, jnp.square(jnp.maximum(f, 0.0)).astype(BF16)

    def ff_out(f):
        return _dot(f, w_ff2_ref[...])

    def embed(rows, h, f):
        h = h + _rms(f, g_post_ffn_ref[...])
        e = (_dot(p_ref[rows, :].astype(BF16), w_ple_proj_ref[...])
             * jax.nn.sigmoid(_dot(h.astype(BF16), w_ple_gate_ref[...]) + b_ple_gate_ref[...]))
        o_ref[rows, :] = h + _rms(e, g_post_ple_ref[...])

    mixed = [out_proj(rows) for rows in groups]
    hf = [ff_in(rows, m) for rows, m in zip(groups, mixed)]
    f2 = [ff_out(f) for _, f in hf]
    for rows, (h, _), f in zip(groups, hf, f2):
        embed(rows, h, f)


def _merge_ffn(x2, ya, yb, p2, params):
    tokens = x2.shape[0]
    tm = MERGE_TILE
    tok_spec = pl.BlockSpec((tm, D_MODEL), lambda t: (t, 0))
    return pl.pallas_call(
        _merge_ffn_kernel,
        grid=(tokens // tm,),
        in_specs=[tok_spec, tok_spec, tok_spec, pl.BlockSpec((tm, PLE_DIM), lambda t: (t, 0))]
                 + [_resident(a.shape) for a in params],
        out_specs=tok_spec,
        out_shape=jax.ShapeDtypeStruct((tokens, D_MODEL), F32),
        compiler_params=pltpu.CompilerParams(
            dimension_semantics=("arbitrary",), vmem_limit_bytes=V7X_VMEM_LIMIT_BYTES),
        name="merge_ffn",
    )(x2, ya, yb, p2, *params)


def _rope_tables(seq):
    pos = jnp.arange(seq, dtype=F32)
    inv = 1.0 / (ROPE_THETA ** (jnp.arange(0, HEAD_DIM, 2, dtype=F32) / HEAD_DIM))
    ang = pos[:, None] * inv[None, :]
    cos = jnp.cos(ang)
    sin = jnp.sin(ang)
    cos_full = jnp.concatenate([cos, cos, cos, cos], axis=-1)
    sin_signed = jnp.concatenate([-sin, -sin, sin, sin], axis=-1)
    return cos_full, sin_signed


def _rotary_layout(w):
    rows = w.shape[0]
    w = w.reshape(rows, HEADS, 2, 2, HEAD_DIM // 2)
    return w.transpose(0, 1, 3, 2, 4).reshape(rows, HEADS * HEAD_WIDTH)


def kernel(x, p, norm_pre_mix, w_in, ln_v_gain, ln_v_bias, w_spatial, b_spatial, lambda_q1, lambda_k1, lambda_q2, lambda_k2, subln_gain, w_branch_a, w_branch_b, w_out, norm_post_mix, norm_pre_ffn, w_ff1, w_ff2, norm_post_ffn, w_ple_proj, w_ple_gate, b_ple_gate, norm_post_ple):
    batch, seq, d_model = x.shape
    depth = w_in.shape[0]
    assert d_model == D_MODEL and depth == 1
    assert seq % ATTN_TILE == 0 and ATTN_TILE % TOKEN_TILE == 0 and TOKEN_TILE % GMLP_BLOCK == 0
    tokens = batch * seq
    nt = seq // TOKEN_TILE
    x2 = x.reshape(tokens, D_MODEL)
    cos, sin_signed = _rope_tables(seq)

    def row(a):
        return a[0].reshape(1, -1).astype(F32)

    w_in_bf = w_in[0].astype(BF16)
    w_mix = jnp.concatenate([
        w_in_bf[:, :2 * D_MODEL],
        _rotary_layout(w_in_bf[:, 2 * D_MODEL:3 * D_MODEL]),
        _rotary_layout(w_in_bf[:, 3 * D_MODEL:4 * D_MODEL]),
        w_in_bf[:, 4 * D_MODEL:MIX_WIDTH]], axis=1)
    b_sp_full = jnp.repeat(b_spatial[0].T.astype(F32), D_MODEL // GMLP_GROUPS, axis=1)

    ya, qt, k, vt = _mix_in(
        x2, row(norm_pre_mix), w_mix, row(ln_v_gain), row(ln_v_bias),
        w_spatial[0].astype(F32), b_sp_full, cos, sin_signed, seq)

    lam_params = jnp.concatenate([lambda_q1, lambda_k1, lambda_q2, lambda_k2], axis=0).astype(F32)
    yb = _diff_attn(qt.reshape(batch, nt, D_MODEL, TOKEN_TILE), k.reshape(batch, seq, D_MODEL),
                    vt.reshape(batch, nt, D_MODEL, TOKEN_TILE), lam_params, row(subln_gain))

    params = (
        row(norm_pre_mix), w_in_bf[:, MIX_WIDTH:], w_branch_a[0].astype(BF16),
        w_branch_b[0].astype(BF16), w_out[0].astype(BF16), row(norm_post_mix),
        row(norm_pre_ffn), w_ff1[0].astype(BF16), w_ff2[0].astype(BF16), row(norm_post_ffn),
        w_ple_proj[0].astype(BF16), w_ple_gate[0].astype(BF16), row(b_ple_gate), row(norm_post_ple),
    )
    out = _merge_ffn(x2, ya, yb.reshape(tokens, D_MODEL), p[0].reshape(tokens, PLE_DIM), params)
    return out.reshape(batch, seq, D_MODEL)
```
